```python
import jax
import jax.numpy as jnp
from jax import lax
import numpy as np

D_MODEL = 4096
BATCH = 2
SEQ = 8192
DEPTH = 1
DEC_BATCH = 32
DEC_SEQ = 32
PAST_LEN = 2048

CHUNK = 64
Q_BLOCK = 128

D_SB = D_MODEL // 2
SB_HEAD_DIM = 128
SB_HEADS = D_SB // SB_HEAD_DIM
SB_SCALE = SB_HEAD_DIM ** -0.5

D_HG = D_MODEL - D_SB
HG_HEAD_DIM = 128
HG_HEADS = D_HG // HG_HEAD_DIM
HG_EXPAND = 128
D_HG_KEY = HG_HEADS * HG_EXPAND

D_MIX = D_SB + D_HG
PROJ_SPLITS = (D_SB, 2 * D_SB, 3 * D_SB, 3 * D_SB + D_HG_KEY, 3 * D_SB + 2 * D_HG_KEY,
               3 * D_SB + 2 * D_HG_KEY + D_HG)
N_PROJ = 3 * D_SB + 2 * D_HG_KEY + 2 * D_HG

N_EXPERTS = 32
TOP_K = 4
D_EXPERT = D_MODEL
SWIGLU_LIMIT = 7.0
SWIGLU_ALPHA = 1.702
MOE_BLOCK = 256

N_MOD = 6
DEEPNORM_ALPHA = (2.0 * DEPTH) ** 0.25
DEEPNORM_BETA = (8.0 * DEPTH) ** -0.25
LN_EPS = 1e-5
RMS_EPS = 1e-6

kernel_name = 'stickbreak_hgrn2_moe_stream_encoder_step'


def layer_norm(x, g, b):
    xf = x.astype(jnp.float32)
    mu = jnp.mean(xf, axis=-1, keepdims=True)
    var = jnp.mean(jnp.square(xf - mu), axis=-1, keepdims=True)
    return ((xf - mu) * lax.rsqrt(var + LN_EPS) * g + b).astype(x.dtype)


def head_rms_norm(o, gain):
    B, T = o.shape[0], o.shape[1]
    of = o.astype(jnp.float32)
    of = of * lax.rsqrt(jnp.mean(jnp.square(of), axis=-1, keepdims=True) + RMS_EPS)
    return of.reshape(B, T, -1) * gain


def ada_modulation(c, w, b):
    m = jax.nn.silu(c) @ w + b
    return jnp.split(m[:, None, :], N_MOD, axis=-1)


def stick_breaking_block(q, q_pos, k, v, k_pos):
    z = jnp.einsum('bhqd,bhkd->bhqk', q, k).astype(jnp.float32) * SB_SCALE
    earlier = k_pos[None, :] < q_pos[:, None]
    log_fail = jnp.where(earlier, jax.nn.log_sigmoid(-z), 0.0)
    after = lax.cumsum(log_fail, axis=3, reverse=True) - log_fail
    w = jnp.where(earlier, jnp.exp(jax.nn.log_sigmoid(z) + after), 0.0)
    return jnp.einsum('bhqk,bhkd->bhqd', w.astype(v.dtype), v)


def stick_breaking_prompt(q, k, v):
    B, H, T, d = q.shape
    nb = T // Q_BLOCK
    q_blocks = q.reshape(B, H, nb, Q_BLOCK, d).transpose(2, 0, 1, 3, 4)
    k_pos = jnp.arange(T)

    def one_block(args):
        q_blk, start = args
        return stick_breaking_block(q_blk, start + jnp.arange(Q_BLOCK), k, v, k_pos)

    o = lax.map(one_block, (q_blocks, jnp.arange(nb) * Q_BLOCK))
    return o.transpose(1, 2, 0, 3, 4).reshape(B, H, T, d)


def hgrn2_recurrence(q, log_f, k, v, s0):
    B, T, H, _ = q.shape
    c = min(CHUNK, T)
    n = T // c

    def to_chunks(a):
        return a.reshape(B, n, c, H, a.shape[-1]).transpose(1, 0, 3, 2, 4)

    incl = jnp.tril(jnp.ones((c, c), dtype=bool))

    def step(S, xs):
        qc, lfc, kc, vc = xs
        b = jnp.cumsum(lfc, axis=2)
        diff = b[:, :, :, None, :] - b[:, :, None, :, :]
        decay = jnp.where(incl[:, :, None], jnp.exp(jnp.minimum(diff, 0.0)), 0.0)
        scores = jnp.einsum('bhtk,bhtsk->bhts', qc, decay * kc[:, :, None, :, :])
        o = (jnp.einsum('bhts,bhsv->bhtv', scores, vc)
             + jnp.einsum('bhtk,bhkv->bhtv', qc * jnp.exp(b), S))
        b_end = b[:, :, -1, :]
        S = (jnp.exp(b_end)[..., None] * S
             + jnp.einsum('bhsk,bhsv->bhkv', kc * jnp.exp(b_end[:, :, None, :] - b), vc))
        return S, o

    S, o = lax.scan(step, s0, (to_chunks(q), to_chunks(log_f), to_chunks(k), to_chunks(v)))
    o = o.transpose(1, 0, 3, 2, 4).reshape(B, T, H, v.shape[-1])
    return o, S


def token_mixer(h, past_k, past_v, s0, lb, w_in, w_out, norm_a, norm_b):
    B, T, _ = h.shape
    q_a, k_a, v_a, q_b, f_b, i_b, g_b = jnp.split(h @ w_in, PROJ_SPLITS, axis=-1)

    def sb_heads(a):
        return a.reshape(B, T, SB_HEADS, SB_HEAD_DIM).transpose(0, 2, 1, 3)

    q_a, k_a, v_a = sb_heads(q_a), sb_heads(k_a), sb_heads(v_a)
    if past_k is None:
        o_a = stick_breaking_prompt(q_a, k_a, v_a)
    else:
        p = past_k.shape[2]
        keys = jnp.concatenate([past_k.astype(k_a.dtype), k_a], axis=2)
        vals = jnp.concatenate([past_v.astype(v_a.dtype), v_a], axis=2)
        o_a = stick_breaking_block(q_a, p + jnp.arange(T), keys, vals, jnp.arange(p + T))
    o_a = head_rms_norm(o_a.transpose(0, 2, 1, 3), norm_a)

    a = f_b.reshape(B, T, HG_HEADS, HG_EXPAND).astype(jnp.float32)
    lb = lb.reshape(HG_HEADS, HG_EXPAND)
    log_f = jnp.logaddexp(jnp.log(lb), jnp.log1p(-lb) + jax.nn.log_sigmoid(a))
    k_b = (1.0 - lb) * jax.nn.sigmoid(-a)
    q_b = q_b.reshape(B, T, HG_HEADS, HG_EXPAND).astype(jnp.float32)
    i_b = i_b.reshape(B, T, HG_HEADS, HG_HEAD_DIM).astype(jnp.float32)
    if s0 is None:
        s0 = jnp.zeros((B, HG_HEADS, HG_EXPAND, HG_HEAD_DIM), jnp.float32)
    o_b, s_new = hgrn2_recurrence(q_b, log_f, k_b, i_b, s0.astype(jnp.float32))
    o_b = head_rms_norm(o_b, norm_b) * jax.nn.silu(g_b.astype(jnp.float32))

    o = jnp.concatenate([o_a, o_b], axis=-1).astype(h.dtype) @ w_out
    return o, k_a, v_a, s_new


def moe_ffn(h, l, router_w, router_b, w_gate, b_gate, w_up, b_up, w_down, b_down):
    B, T, D = h.shape
    n_tok = B * T
    xt = h.reshape(n_tok, D)
    logits = (xt @ router_w + router_b).astype(jnp.float32)
    top_logit, top_e = lax.top_k(logits, TOP_K)
    gates = jax.nn.softmax(top_logit, axis=-1)
    nk = n_tok * TOP_K
    flat_e = top_e.reshape(nk)
    flat_tok = jnp.repeat(jnp.arange(n_tok, dtype=jnp.int32), TOP_K)
    flat_gate = gates.reshape(nk)
    order = jnp.argsort(flat_e)
    e_sorted = flat_e[order]
    counts = jnp.bincount(flat_e, length=N_EXPERTS)
    padded = (counts + MOE_BLOCK - 1) // MOE_BLOCK * MOE_BLOCK
    padded_end = jnp.cumsum(padded)
    rank = jnp.arange(nk, dtype=jnp.int32) - (jnp.cumsum(counts) - counts)[e_sorted]
    dest = (padded_end - padded)[e_sorted] + rank
    n_blocks = -(-nk // MOE_BLOCK) + N_EXPERTS
    slot_tok = jnp.zeros((n_blocks * MOE_BLOCK,), jnp.int32).at[dest].set(flat_tok[order])
    slot_gate = jnp.zeros((n_blocks * MOE_BLOCK,), jnp.float32).at[dest].set(flat_gate[order])
    block_expert = jnp.minimum(
        jnp.searchsorted(padded_end, jnp.arange(n_blocks) * MOE_BLOCK, side='right'), N_EXPERTS - 1)

    def block(y, xs):
        tok, g, e = xs
        xb = xt[tok]
        gate = jnp.minimum(xb @ w_gate[l, e] + b_gate[l, e], SWIGLU_LIMIT)
        up = jnp.clip(xb @ w_up[l, e] + b_up[l, e], -SWIGLU_LIMIT, SWIGLU_LIMIT)
        act = (up + 1.0) * gate * jax.nn.sigmoid(SWIGLU_ALPHA * gate)
        out = (act @ w_down[l, e] + b_down[l, e]).astype(jnp.float32)
        return y.at[tok].add(out * g[:, None]), None

    y, _ = lax.scan(block, jnp.zeros((n_tok, D), jnp.float32),
                    (slot_tok.reshape(n_blocks, MOE_BLOCK), slot_gate.reshape(n_blocks, MOE_BLOCK),
                     block_expert))
    return y.reshape(B, T, D).astype(h.dtype)


def encoder_layer(x, c, past_k, past_v, s0, l, lb, w_ada, b_ada, w_in, w_out, norm_a, norm_b,
                  ln1_g, ln1_b, ln2_g, ln2_b, router_w, router_b,
                  w_gate, b_gate, w_up, b_up, w_down, b_down):
    shift1, scale1, gate1, shift2, scale2, gate2 = ada_modulation(c, w_ada, b_ada)
    mix, k_rows, v_rows, s_new = token_mixer(x * (1.0 + scale1) + shift1, past_k, past_v, s0, lb,
                                             w_in, w_out, norm_a, norm_b)
    x = layer_norm(DEEPNORM_ALPHA * x + gate1 * mix, ln1_g, ln1_b)
    ffn = moe_ffn(x * (1.0 + scale2) + shift2, l, router_w, router_b,
                  w_gate, b_gate, w_up, b_up, w_down, b_down)
    x = layer_norm(DEEPNORM_ALPHA * x + gate2 * ffn, ln2_g, ln2_b)
    return x, k_rows, v_rows, s_new


def setup_inputs(seed: int = 0) -> dict:
    key = jax.random.key(seed)
    ks = jax.random.split(key, 26)
    d = D_MODEL

    def nrm(k, shape, scale):
        return jax.random.normal(k, shape, jnp.float32) * scale

    return {
        'x_prompt': nrm(ks[0], (BATCH, SEQ, d), 1.0),
        'x_sample': nrm(ks[1], (DEC_BATCH, DEC_SEQ, d), 1.0),
        'c_prompt': nrm(ks[2], (BATCH, d), 1.0),
        'c_sample': nrm(ks[3], (DEC_BATCH, d), 1.0),
        'cache_sb_k': nrm(ks[4], (DEPTH, DEC_BATCH, SB_HEADS, PAST_LEN, SB_HEAD_DIM), 1.0),
        'cache_sb_v': nrm(ks[5], (DEPTH, DEC_BATCH, SB_HEADS, PAST_LEN, SB_HEAD_DIM), 1.0),
        'state_hgrn': nrm(ks[6], (DEPTH, DEC_BATCH, HG_HEADS, HG_EXPAND, HG_HEAD_DIM), 0.3),
        'w_ada': nrm(ks[7], (DEPTH, d, N_MOD * d), 0.5 * d ** -0.5),
        'b_ada': nrm(ks[8], (DEPTH, N_MOD * d), 0.01),
        'w_in': nrm(ks[9], (DEPTH, d, N_PROJ), d ** -0.5),
        'w_out': nrm(ks[10], (DEPTH, D_MIX, d), DEEPNORM_BETA * D_MIX ** -0.5),
        'norm_a': 1.0 + nrm(ks[11], (DEPTH, D_SB), 0.02),
        'norm_b': 1.0 + nrm(ks[12], (DEPTH, D_HG), 0.02),
        'lb_logits': nrm(ks[13], (DEPTH + 1, D_HG_KEY), 1.0),
        'ln1_g': 1.0 + nrm(ks[14], (DEPTH, d), 0.02),
        'ln1_b': nrm(ks[15], (DEPTH, d), 0.02),
        'ln2_g': 1.0 + nrm(ks[16], (DEPTH, d), 0.02),
        'ln2_b': nrm(ks[17], (DEPTH, d), 0.02),
        'router_w': nrm(ks[18], (DEPTH, d, N_EXPERTS), d ** -0.5),
        'router_b': nrm(ks[19], (DEPTH, N_EXPERTS), 0.01),
        'w_gate': nrm(ks[20], (DEPTH, N_EXPERTS, d, D_EXPERT), d ** -0.5),
        'b_gate': nrm(ks[21], (DEPTH, N_EXPERTS, D_EXPERT), 0.01),
        'w_up': nrm(ks[22], (DEPTH, N_EXPERTS, d, D_EXPERT), d ** -0.5),
        'b_up': nrm(ks[23], (DEPTH, N_EXPERTS, D_EXPERT), 0.01),
        'w_down': nrm(ks[24], (DEPTH, N_EXPERTS, D_EXPERT, d), DEEPNORM_BETA * D_EXPERT ** -0.5),
        'b_down': nrm(ks[25], (DEPTH, N_EXPERTS, d), 0.01),
    }


def reference(x_prompt, x_sample, c_prompt, c_sample, cache_sb_k, cache_sb_v, state_hgrn,
              w_ada, b_ada, w_in, w_out, norm_a, norm_b, lb_logits,
              ln1_g, ln1_b, ln2_g, ln2_b, router_w, router_b,
              w_gate, b_gate, w_up, b_up, w_down, b_down):
    lower_bounds = jnp.cumsum(jax.nn.softmax(lb_logits.astype(jnp.float32), axis=0), axis=0)
    y_prompt, y_sample = x_prompt, x_sample
    kp, vp, sp, ksm, vsm, ssm = [], [], [], [], [], []
    for l in range(DEPTH):
        y_prompt, k_rows, v_rows, s_new = encoder_layer(
            y_prompt, c_prompt, None, None, None, l, lower_bounds[l],
            w_ada[l], b_ada[l], w_in[l], w_out[l], norm_a[l], norm_b[l],
            ln1_g[l], ln1_b[l], ln2_g[l], ln2_b[l], router_w[l], router_b[l],
            w_gate, b_gate, w_up, b_up, w_down, b_down)
        kp.append(k_rows)
        vp.append(v_rows)
        sp.append(s_new)
        y_sample, k_rows, v_rows, s_new = encoder_layer(
            y_sample, c_sample, cache_sb_k[l], cache_sb_v[l], state_hgrn[l], l, lower_bounds[l],
            w_ada[l], b_ada[l], w_in[l], w_out[l], norm_a[l], norm_b[l],
            ln1_g[l], ln1_b[l], ln2_g[l], ln2_b[l], router_w[l], router_b[l],
            w_gate, b_gate, w_up, b_up, w_down, b_down)
        ksm.append(k_rows)
        vsm.append(v_rows)
        ssm.append(s_new)
    k_prompt = jnp.stack(kp)
    v_prompt = jnp.stack(vp)
    state_prompt = jnp.stack(sp)
    k_sample = jnp.stack(ksm)
    v_sample = jnp.stack(vsm)
    state_sample = jnp.stack(ssm)
    return (y_prompt, y_sample, k_prompt, v_prompt, state_prompt, k_sample, v_sample, state_sample)
```

```python
import functools

import jax
import jax.numpy as jnp
from jax import lax
from jax.experimental import pallas as pl
from jax.experimental.pallas import tpu as pltpu

HEAD_DIM = 128
TOP_K = 4
SWIGLU_LIMIT = 7.0
SWIGLU_ALPHA = 1.702
LN_EPS = 1e-5
RMS_EPS = 1e-6
N_MOD = 6

LANES = 128
SB_TILE = 256
SB_DEAD = -104.0
HG_GROUP = 16
MOE_TM = 256
VMEM_LIMIT = 56 * 1024 * 1024

F32 = jnp.float32
BF16 = jnp.bfloat16


def _params(*sem):
    return pltpu.CompilerParams(dimension_semantics=sem, vmem_limit_bytes=VMEM_LIMIT)


def _dot(a, b):
    return jnp.dot(a, b, preferred_element_type=F32)


def _dot_nt(a, b):
    return lax.dot_general(a, b, (((1,), (1,)), ((), ())), preferred_element_type=F32)


def _dot_tn(a, b):
    return lax.dot_general(a, b, (((0,), (0,)), ((), ())), preferred_element_type=F32)


def _split3(x):
    hi = x.astype(BF16)
    r1 = x - hi.astype(F32)
    mid = r1.astype(BF16)
    lo = (r1 - mid.astype(F32)).astype(BF16)
    return hi, mid, lo


def _dot_exact_rhs(x, m):
    hi, mid, lo = _split3(x)
    return _dot(m, hi) + _dot(m, mid) + _dot(m, lo)


def _softplus(z):
    return jnp.maximum(z, 0.0) + jnp.log(1.0 + jnp.exp(-jnp.abs(z)))


def _ada_kernel(c_ref, w_ref, b_ref, o_ref):
    c = c_ref[...]
    s = (c * jax.nn.sigmoid(c)).astype(BF16)
    o_ref[...] = _dot(s, w_ref[...].astype(BF16)) + b_ref[...]


def ada_modulation(c, w, b):
    r, d = c.shape
    n = w.shape[1]
    tn = min(n, 512)
    return pl.pallas_call(
        _ada_kernel,
        grid=(n // tn,),
        in_specs=[pl.BlockSpec((r, d), lambda j: (0, 0)),
                  pl.BlockSpec((d, tn), lambda j: (0, j)),
                  pl.BlockSpec((1, tn), lambda j: (0, j))],
        out_specs=pl.BlockSpec((r, tn), lambda j: (0, j)),
        out_shape=jax.ShapeDtypeStruct((r, n), F32),
        compiler_params=_params("arbitrary"),
        name="ada_modulation",
    )(c, w, b.reshape(1, n))


def _modulate_kernel(x_ref, sc_ref, sh_ref, o_ref):
    o_ref[...] = (x_ref[...] * (1.0 + sc_ref[...]) + sh_ref[...]).astype(o_ref.dtype)


def _row_tiles(b, t):
    tt = min(t, 512)
    bb = max(1, min(b, 256 // tt)) if tt < 256 else 1
    return bb, tt


def modulate(x, scale, shift):
    b, t, d = x.shape
    bb, tt = _row_tiles(b, t)
    mod_spec = pl.BlockSpec((bb, 1, d), lambda i, j: (i, 0, 0))
    return pl.pallas_call(
        _modulate_kernel,
        grid=(b // bb, t // tt),
        in_specs=[pl.BlockSpec((bb, tt, d), lambda i, j: (i, j, 0)), mod_spec, mod_spec],
        out_specs=pl.BlockSpec((bb, tt, d), lambda i, j: (i, j, 0)),
        out_shape=jax.ShapeDtypeStruct((b, t, d), BF16),
        compiler_params=_params("parallel", "parallel"),
        name="modulate",
    )(x, scale, shift)


def _proj_kernel(h_ref, w_ref, *o_refs, scale):
    bb, tt, d = h_ref.shape
    acc = _dot(h_ref[...].reshape(bb * tt, d), w_ref[...])
    if scale != 1.0:
        acc = acc * scale
    heads = o_refs[0].shape[1]
    for hh in range(heads):
        blk = acc[:, hh * HEAD_DIM:(hh + 1) * HEAD_DIM].reshape(bb, tt, HEAD_DIM)
        for o_ref in o_refs:
            o_ref[:, hh] = blk.astype(o_ref.dtype)


def project_heads(h, w, out_dtypes, scale=1.0):
    b, t, d = h.shape
    n = w.shape[1]
    heads = n // HEAD_DIM
    bb, tt = _row_tiles(b, t)
    tn = min(n, 1024)
    hpb = tn // HEAD_DIM
    out_spec = pl.BlockSpec((bb, hpb, tt, HEAD_DIM), lambda i, j, k: (i, k, j, 0))
    outs = pl.pallas_call(
        functools.partial(_proj_kernel, scale=scale),
        grid=(b // bb, t // tt, n // tn),
        in_specs=[pl.BlockSpec((bb, tt, d), lambda i, j, k: (i, j, 0)),
                  pl.BlockSpec((d, tn), lambda i, j, k: (0, k))],
        out_specs=[out_spec] * len(out_dtypes),
        out_shape=[jax.ShapeDtypeStruct((b, heads, t, HEAD_DIM), dt) for dt in out_dtypes],
        compiler_params=_params("parallel", "parallel", "arbitrary"),
        name="project_heads",
    )(h, w)
    return outs


def _sb_block(q, k, v, u_ext, carry, masked):
    tk = k.shape[0]
    z = _dot_nt(q, k)
    lf = -_softplus(z)
    if masked:
        row = lax.broadcasted_iota(jnp.int32, z.shape, 0)
        col = lax.broadcasted_iota(jnp.int32, z.shape, 1)
        earlier = col < row
        lf = jnp.where(earlier, lf, 0.0)
    hi = lf.astype(BF16)
    lo = (lf - hi.astype(F32)).astype(BF16)
    cs = _dot(hi, u_ext) + _dot(lo, u_ext)
    if tk >= LANES:
        later = jnp.concatenate([carry] * (tk // LANES), axis=1)
    else:
        later = carry[:, :tk]
    after = cs[:, LANES:] + later
    w = jnp.exp(z + lf + after)
    if masked:
        w = jnp.where(earlier, w, 0.0)
    out = _dot(w.astype(BF16), v)
    return out, carry + cs[:, :LANES]


def _sb_kernel(q_ref, kd_ref, vd_ref, kp_ref, vp_ref, ud_ref, up_ref, gain_ref, o_ref,
               acc_ref, carry_ref, *, past_blocks):
    tq = q_ref.shape[2]
    tk = up_ref.shape[0]
    q = q_ref[0, 0]
    out, carry = _sb_block(q, kd_ref[0, 0].astype(BF16), vd_ref[0, 0].astype(BF16), ud_ref[...],
                           jnp.zeros((tq, LANES), F32), masked=True)
    acc_ref[...] = out
    carry_ref[...] = carry
    n_past = pl.program_id(2) * (tq // tk) if past_blocks is None else past_blocks

    def cond(state):
        kb, alive = state
        return jnp.logical_and(kb >= 0, alive)

    def body(state):
        kb, _ = state
        start = pl.multiple_of(kb * tk, tk)
        k = kp_ref[0, 0, pl.ds(start, tk), :].astype(BF16)
        v = vp_ref[0, 0, pl.ds(start, tk), :].astype(BF16)
        out, carry = _sb_block(q, k, v, up_ref[...], carry_ref[...], masked=False)
        acc_ref[...] += out
        carry_ref[...] = carry
        return kb - 1, jnp.max(carry) > SB_DEAD

    lax.while_loop(cond, body, (n_past - 1, jnp.max(carry) > SB_DEAD))
    o = acc_ref[...]
    o = o * lax.rsqrt(jnp.mean(o * o, axis=-1, keepdims=True) + RMS_EPS)
    o_ref[0] = (o * gain_ref[0]).astype(o_ref.dtype)


def _suffix_matrix(tk):
    j = jnp.arange(tk)[:, None]
    c = jnp.arange(LANES + tk)[None, :]
    return jnp.where((c < LANES) | (j > c - LANES), 1.0, 0.0).astype(BF16)


def stick_breaking(q, k, v, gain, past_k=None, past_v=None):
    b, h, t, d = q.shape
    if past_k is None:
        tq = min(t, SB_TILE)
        tk = tq
        kp, vp, past_blocks = k, v, None
    else:
        tq = t
        tk = min(past_k.shape[2], SB_TILE)
        kp, vp, past_blocks = past_k, past_v, past_k.shape[2] // tk
    p = kp.shape[2]
    tile_spec = pl.BlockSpec((1, 1, tq, d), lambda i, j, m: (i, j, m, 0))
    past_spec = pl.BlockSpec((1, 1, p, d), lambda i, j, m: (i, j, 0, 0))
    return pl.pallas_call(
        functools.partial(_sb_kernel, past_blocks=past_blocks),
        grid=(b, h, t // tq),
        in_specs=[tile_spec, tile_spec, tile_spec, past_spec, past_spec,
                  pl.BlockSpec((tq, tq + LANES), lambda i, j, m: (0, 0)),
                  pl.BlockSpec((tk, tk + LANES), lambda i, j, m: (0, 0)),
                  pl.BlockSpec((1, 1, d), lambda i, j, m: (j, 0, 0))],
        out_specs=pl.BlockSpec((1, tq, d), lambda i, j, m: (i, m, j)),
        out_shape=jax.ShapeDtypeStruct((b, t, h * d), BF16),
        scratch_shapes=[pltpu.VMEM((tq, LANES), F32), pltpu.VMEM((tq, LANES), F32)],
        compiler_params=_params("parallel", "parallel", "arbitrary"),
        name="stick_breaking",
    )(q, k, v, kp, vp, _suffix_matrix(tq), _suffix_matrix(tk), gain)


def _hgrn_kernel(q_ref, a_ref, i_ref, g_ref, lb_ref, s0_ref, gain_ref, cum_ref, tot_ref, ones_ref,
                 o_ref, s_ref, st_ref, qd_ref, kd_ref, dec_ref, od_ref):
    tt = q_ref.shape[2]
    ng = tt // HG_GROUP
    ti = pl.program_id(2)

    @pl.when(ti == 0)
    def _():
        st_ref[...] = s0_ref[0, 0].T

    lb = lb_ref[0]
    a = a_ref[0, 0]
    log_sig = -_softplus(-a)
    x = jnp.log(lb)
    y = jnp.log1p(-lb) + log_sig
    log_f = jnp.maximum(x, y) + jnp.log(1.0 + jnp.exp(-jnp.abs(x - y)))
    kk = (1.0 - lb) * jax.nn.sigmoid(-a)
    q = q_ref[0, 0]
    v = i_ref[0, 0]

    cum = cum_ref[...]
    tot = tot_ref[...]
    rows = cum.shape[0]
    b_parts, e_parts = [], []
    for r in range(tt // rows):
        blk = log_f[r * rows:(r + 1) * rows]
        b_parts.append(_dot_exact_rhs(blk, cum))
        e_parts.append(_dot_exact_rhs(blk, tot))
    bcum = jnp.concatenate(b_parts, axis=0) if len(b_parts) > 1 else b_parts[0]
    bend = jnp.concatenate(e_parts, axis=0) if len(e_parts) > 1 else e_parts[0]
    qd_ref[...] = q * jnp.exp(bcum)
    kd_ref[...] = kk * jnp.exp(bend - bcum)
    dec_ref[...] = jnp.exp(bend)

    b3 = bcum.reshape(ng, HG_GROUP, HEAD_DIM)
    q3 = q.reshape(ng, HG_GROUP, HEAD_DIM)
    k3 = kk.reshape(ng, HG_GROUP, HEAD_DIM)
    v3 = v.reshape(ng, HG_GROUP, HEAD_DIM)
    pos = lax.broadcasted_iota(jnp.int32, (ng, HG_GROUP, HEAD_DIM), 1)
    ones = ones_ref[...]
    od = jnp.zeros((tt, HEAD_DIM), F32)
    for s in range(HG_GROUP):
        diff = b3 - b3[:, s:s + 1, :]
        wgt = jnp.where(pos >= s, jnp.exp(jnp.minimum(diff, 0.0)), 0.0)
        prod = (q3 * k3[:, s:s + 1, :] * wgt).reshape(tt, HEAD_DIM)
        score = _dot(prod.astype(BF16), ones)
        vs = jnp.broadcast_to(v3[:, s:s + 1, :], (ng, HG_GROUP, HEAD_DIM)).reshape(tt, HEAD_DIM)
        od = od + score * vs
    od_ref[...] = od

    gain = gain_ref[0]

    def step(i, carry):
        r0 = pl.multiple_of(i * HG_GROUP, HG_GROUP)
        rows_i = pl.ds(r0, HG_GROUP)
        st = st_ref[...]
        o = od_ref[rows_i, :] + _dot_nt(qd_ref[rows_i, :].astype(BF16), st.astype(BF16))
        upd = _dot_tn(i_ref[0, 0, rows_i, :].astype(BF16), kd_ref[rows_i, :].astype(BF16))
        st_ref[...] = st * dec_ref[pl.ds(r0, 1), :] + upd
        o = o * lax.rsqrt(jnp.mean(o * o, axis=-1, keepdims=True) + RMS_EPS) * gain
        gg = g_ref[0, 0, rows_i, :]
        o_ref[0, rows_i, :] = (o * (gg * jax.nn.sigmoid(gg))).astype(o_ref.dtype)
        return carry

    lax.fori_loop(0, ng, step, 0)

    @pl.when(ti == pl.num_programs(2) - 1)
    def _():
        s_ref[0, 0] = st_ref[...].T


def _group_matrices(rows):
    r = jnp.arange(rows)
    same = (r[:, None] // HG_GROUP) == (r[None, :] // HG_GROUP)
    cum = jnp.where(same & (r[None, :] <= r[:, None]), 1.0, 0.0).astype(BF16)
    tot = jnp.where(same, 1.0, 0.0).astype(BF16)
    return cum, tot


def hgrn2(q, a, i, g, lb, s0, gain):
    b, h, t, d = q.shape
    tt = min(t, 1024)
    rows = min(tt, LANES)
    cum, tot = _group_matrices(rows)
    ones = jnp.ones((d, d), BF16)
    seq_spec = pl.BlockSpec((1, 1, tt, d), lambda bi, hi, ti: (bi, hi, ti, 0))
    head_spec = pl.BlockSpec((1, 1, d), lambda bi, hi, ti: (hi, 0, 0))
    state_spec = pl.BlockSpec((1, 1, d, d), lambda bi, hi, ti: (bi, hi, 0, 0))
    const = lambda shape: pl.BlockSpec(shape, lambda bi, hi, ti: (0, 0))
    return pl.pallas_call(
        _hgrn_kernel,
        grid=(b, h, t // tt),
        in_specs=[seq_spec, seq_spec, seq_spec, seq_spec, head_spec, state_spec, head_spec,
                  const((rows, rows)), const((rows, rows)), const((d, d))],
        out_specs=[pl.BlockSpec((1, tt, d), lambda bi, hi, ti: (bi, ti, hi)), state_spec],
        out_shape=[jax.ShapeDtypeStruct((b, t, h * d), BF16),
                   jax.ShapeDtypeStruct((b, h, d, d), F32)],
        scratch_shapes=[pltpu.VMEM((d, d), F32)] + [pltpu.VMEM((tt, d), F32)] * 4,
        compiler_params=_params("parallel", "parallel", "arbitrary"),
        name="hgrn2",
    )(q, a, i, g, lb, s0, gain, cum, tot, ones)


def _layer_norm(x, g, b):
    mu = jnp.mean(x, axis=-1, keepdims=True)
    xc = x - mu
    var = jnp.mean(xc * xc, axis=-1, keepdims=True)
    return xc * lax.rsqrt(var + LN_EPS) * g + b


def _out_kernel(oa_ref, ob_ref, wa_ref, wb_ref, x_ref, gate_ref, sc_ref, sh_ref, lng_ref, lnb_ref,
                rw_ref, rb_ref, x1_ref, h2_ref, lg_ref, acc_ref, *, alpha):
    bb, tt, dh = oa_ref.shape
    tn = wa_ref.shape[1]
    n = pl.program_id(2)
    part = (_dot(oa_ref[...].reshape(bb * tt, dh), wa_ref[...])
            + _dot(ob_ref[...].reshape(bb * tt, dh), wb_ref[...]))
    acc_ref[n] = part

    @pl.when(n == pl.num_programs(2) - 1)
    def _():
        d = x_ref.shape[2]
        mix = jnp.concatenate([acc_ref[k] for k in range(d // tn)], axis=1).reshape(bb, tt, d)
        x1 = _layer_norm(alpha * x_ref[...] + gate_ref[...] * mix, lng_ref[...], lnb_ref[...])
        x1_ref[...] = x1
        h2 = x1 * (1.0 + sc_ref[...]) + sh_ref[...]
        h2_ref[...] = h2.astype(h2_ref.dtype)
        hh, hm, hl = _split3(h2.reshape(bb * tt, d))
        w = rw_ref[...]
        wh = w.astype(BF16)
        wl = (w - wh.astype(F32)).astype(BF16)
        lg = (_dot(hh, wh) + (_dot(hm, wh) + _dot(hh, wl))) + (_dot(hl, wh) + _dot(hm, wl))
        lg_ref[...] = (lg + rb_ref[...]).reshape(bb, tt, lg.shape[-1])


def out_projection(oa, ob, wa, wb, x, gate, scale, shift, ln_g, ln_b, router_w, router_b, alpha):
    b, t, dh = oa.shape
    d = x.shape[2]
    e = router_w.shape[1]
    bb, tt = _row_tiles(b, t)
    tt = min(tt, 256)
    tn = min(d, 512)
    row = lambda w_: pl.BlockSpec((bb, tt, w_), lambda i, j, n: (i, j, 0))
    mod = pl.BlockSpec((bb, 1, d), lambda i, j, n: (i, 0, 0))
    vec = lambda w_: pl.BlockSpec((1, w_), lambda i, j, n: (0, 0))
    wspec = pl.BlockSpec((dh, tn), lambda i, j, n: (0, n))
    return pl.pallas_call(
        functools.partial(_out_kernel, alpha=alpha),
        grid=(b // bb, t // tt, d // tn),
        in_specs=[row(dh), row(dh), wspec, wspec, row(d), mod, mod, mod, vec(d), vec(d),
                  pl.BlockSpec((d, e), lambda i, j, n: (0, 0)), vec(e)],
        out_specs=[row(d), row(d), row(e)],
        out_shape=[jax.ShapeDtypeStruct((b, t, d), F32), jax.ShapeDtypeStruct((b, t, d), F32),
                   jax.ShapeDtypeStruct((b, t, e), F32)],
        scratch_shapes=[pltpu.VMEM((d // tn, bb * tt, tn), F32)],
        compiler_params=_params("parallel", "parallel", "arbitrary"),
        name="out_projection",
    )(oa, ob, wa, wb, x, gate, scale, shift, ln_g.reshape(1, d), ln_b.reshape(1, d),
      router_w, router_b.reshape(1, e))


GATHER_ROWS = 256


def _gather_kernel(tok_ref, src_ref, dst_ref, sem):
    base = pl.program_id(0) * GATHER_ROWS

    def copy(r):
        return pltpu.make_async_copy(src_ref.at[pl.ds(tok_ref[base + r], 1), :],
                                     dst_ref.at[pl.ds(base + r, 1), :], sem)

    def start(r, c):
        copy(r).start()
        return c

    def wait(r, c):
        copy(r).wait()
        return c

    lax.fori_loop(0, GATHER_ROWS, start, 0)
    lax.fori_loop(0, GATHER_ROWS, wait, 0)


def gather_rows(src, tok):
    r = tok.shape[0]
    return pl.pallas_call(
        _gather_kernel,
        grid_spec=pltpu.PrefetchScalarGridSpec(
            num_scalar_prefetch=1, grid=(r // GATHER_ROWS,),
            in_specs=[pl.BlockSpec(memory_space=pl.ANY)],
            out_specs=pl.BlockSpec(memory_space=pl.ANY),
            scratch_shapes=[pltpu.SemaphoreType.DMA(())]),
        out_shape=jax.ShapeDtypeStruct((r,) + src.shape[1:], src.dtype),
        compiler_params=_params("arbitrary"),
        name="gather_rows",
    )(tok, src)


def _gate_up_kernel(be_ref, nb_ref, x_ref, wg_ref, bg_ref, wu_ref, bu_ref, o_ref):
    @pl.when(pl.program_id(1) < nb_ref[0])
    def _():
        x = x_ref[...].astype(BF16)
        gate = jnp.minimum(_dot(x, wg_ref[0]) + bg_ref[0], SWIGLU_LIMIT)
        up = jnp.clip(_dot(x, wu_ref[0]) + bu_ref[0], -SWIGLU_LIMIT, SWIGLU_LIMIT)
        o_ref[...] = ((up + 1.0) * gate * jax.nn.sigmoid(SWIGLU_ALPHA * gate)).astype(o_ref.dtype)

    @pl.when(pl.program_id(1) >= nb_ref[0])
    def _():
        o_ref[...] = jnp.zeros_like(o_ref)


def _down_kernel(be_ref, nb_ref, a_ref, w_ref, b_ref, g_ref, o_ref):
    @pl.when(pl.program_id(1) < nb_ref[0])
    def _():
        o_ref[...] = (_dot(a_ref[...], w_ref[0]) + b_ref[0]) * g_ref[...]

    @pl.when(pl.program_id(1) >= nb_ref[0])
    def _():
        o_ref[...] = jnp.zeros_like(o_ref)


def expert_ffn(xs, slot_gate, block_expert, n_used, w_gate, b_gate, w_up, b_up, w_down, b_down):
    r, d = xs.shape
    f = w_gate.shape[2]
    nb = r // MOE_TM
    tn = min(f, 1024)
    wspec = lambda k, n_: pl.BlockSpec((1, k, n_), lambda j, i, be, nu: (be[i], 0, j))
    bspec = lambda n_: pl.BlockSpec((1, 1, n_), lambda j, i, be, nu: (be[i], 0, j))
    act = pl.pallas_call(
        _gate_up_kernel,
        grid_spec=pltpu.PrefetchScalarGridSpec(
            num_scalar_prefetch=2, grid=(f // tn, nb),
            in_specs=[pl.BlockSpec((MOE_TM, d), lambda j, i, be, nu: (i, 0)),
                      wspec(d, tn), bspec(tn), wspec(d, tn), bspec(tn)],
            out_specs=pl.BlockSpec((MOE_TM, tn), lambda j, i, be, nu: (i, j))),
        out_shape=jax.ShapeDtypeStruct((r, f), BF16),
        compiler_params=_params("parallel", "arbitrary"),
        name="expert_gate_up",
    )(block_expert, n_used, xs, w_gate, b_gate, w_up, b_up)
    tn = min(d, 1024)
    return pl.pallas_call(
        _down_kernel,
        grid_spec=pltpu.PrefetchScalarGridSpec(
            num_scalar_prefetch=2, grid=(d // tn, nb),
            in_specs=[pl.BlockSpec((MOE_TM, f), lambda j, i, be, nu: (i, 0)),
                      wspec(f, tn), bspec(tn),
                      pl.BlockSpec((MOE_TM, 1), lambda j, i, be, nu: (i, 0))],
            out_specs=pl.BlockSpec((MOE_TM, tn), lambda j, i, be, nu: (i, j))),
        out_shape=jax.ShapeDtypeStruct((r, d), F32),
        compiler_params=_params("parallel", "arbitrary"),
        name="expert_down",
    )(block_expert, n_used, act, w_down, b_down, slot_gate.reshape(r, 1))


COMBINE_TOKENS = 32


def _combine_kernel(pos_ref, ys_ref, x1_ref, gate_ref, lng_ref, lnb_ref, o_ref, buf_ref, sem,
                    *, alpha, tok_offset):
    bb, tt, d = x1_ref.shape
    nt = bb * tt
    base = (tok_offset + (pl.program_id(0) * pl.num_programs(1) + pl.program_id(1)) * nt) * TOP_K

    def copy(r):
        t, k = r // TOP_K, r % TOP_K
        return pltpu.make_async_copy(ys_ref.at[pl.ds(pos_ref[base + r], 1), :],
                                     buf_ref.at[k, pl.ds(t, 1), :], sem)

    def start(r, c):
        copy(r).start()
        return c

    def wait(r, c):
        copy(r).wait()
        return c

    lax.fori_loop(0, nt * TOP_K, start, 0)
    lax.fori_loop(0, nt * TOP_K, wait, 0)
    ffn = (buf_ref[0] + buf_ref[1]) + (buf_ref[2] + buf_ref[3])
    y = alpha * x1_ref[...] + gate_ref[...] * ffn.reshape(bb, tt, d)
    o_ref[...] = _layer_norm(y, lng_ref[...], lnb_ref[...])


def combine(ys, pos, x1, gate, ln_g, ln_b, alpha, tok_offset):
    b, t, d = x1.shape
    tt = min(t, COMBINE_TOKENS)
    bb = 1
    row = pl.BlockSpec((bb, tt, d), lambda i, j, p: (i, j, 0))
    mod = pl.BlockSpec((bb, 1, d), lambda i, j, p: (i, 0, 0))
    vec = pl.BlockSpec((1, d), lambda i, j, p: (0, 0))
    return pl.pallas_call(
        functools.partial(_combine_kernel, alpha=alpha, tok_offset=tok_offset),
        grid_spec=pltpu.PrefetchScalarGridSpec(
            num_scalar_prefetch=1, grid=(b // bb, t // tt),
            in_specs=[pl.BlockSpec(memory_space=pl.ANY), row, mod, vec, vec],
            out_specs=row,
            scratch_shapes=[pltpu.VMEM((TOP_K, bb * tt, d), F32), pltpu.SemaphoreType.DMA(())]),
        out_shape=jax.ShapeDtypeStruct((b, t, d), F32),
        compiler_params=_params("arbitrary", "arbitrary"),
        name="moe_combine",
    )(pos, ys, x1, gate, ln_g.reshape(1, d), ln_b.reshape(1, d))


def _route(logits):
    n_tok, n_exp = logits.shape
    top_logit, top_e = lax.top_k(logits, TOP_K)
    gates = jax.nn.softmax(top_logit, axis=-1)
    nk = n_tok * TOP_K
    flat_e = top_e.reshape(nk).astype(jnp.int32)
    flat_tok = jnp.repeat(jnp.arange(n_tok, dtype=jnp.int32), TOP_K)
    order = jnp.argsort(flat_e)
    e_sorted = flat_e[order]
    counts = jnp.zeros((n_exp,), jnp.int32).at[flat_e].add(1)
    padded = (counts + MOE_TM - 1) // MOE_TM * MOE_TM
    padded_end = jnp.cumsum(padded)
    rank = jnp.arange(nk, dtype=jnp.int32) - (jnp.cumsum(counts) - counts)[e_sorted]
    dest = ((padded_end - padded)[e_sorted] + rank).astype(jnp.int32)
    n_blocks = -(-nk // MOE_TM) + n_exp
    slot_tok = jnp.zeros((n_blocks * MOE_TM,), jnp.int32).at[dest].set(flat_tok[order])
    slot_gate = jnp.zeros((n_blocks * MOE_TM,), F32).at[dest].set(gates.reshape(nk)[order])
    block_expert = jnp.minimum(
        jnp.searchsorted(padded_end, jnp.arange(n_blocks, dtype=jnp.int32) * MOE_TM, side='right'),
        n_exp - 1).astype(jnp.int32)
    n_used = (padded_end[-1:] // MOE_TM).astype(jnp.int32)
    pos = jnp.zeros((nk,), jnp.int32).at[order].set(dest)
    return slot_tok, slot_gate, block_expert, n_used, pos


def kernel(x_prompt, x_sample, c_prompt, c_sample, cache_sb_k, cache_sb_v, state_hgrn,
           w_ada, b_ada, w_in, w_out, norm_a, norm_b, lb_logits,
           ln1_g, ln1_b, ln2_g, ln2_b, router_w, router_b,
           w_gate, b_gate, w_up, b_up, w_down, b_down):
    depth = w_ada.shape[0]
    d = x_prompt.shape[-1]
    alpha = (2.0 * depth) ** 0.25
    d_sb = cache_sb_k.shape[2] * HEAD_DIM
    hg_heads = state_hgrn.shape[2]
    d_hk = hg_heads * state_hgrn.shape[3]
    d_hg = hg_heads * state_hgrn.shape[4]
    sb_scale = HEAD_DIM ** -0.5
    n_exp = router_w.shape[-1]
    lower_bounds = jnp.cumsum(jax.nn.softmax(lb_logits.astype(F32), axis=0), axis=0)
    streams = [x_prompt, x_sample]
    conds = [c_prompt, c_sample]
    n_rows = [c.shape[0] for c in conds]
    pad = (-sum(n_rows)) % 8
    c_all = jnp.concatenate(conds + [jnp.zeros((pad, d), F32)], axis=0)
    outs = [[] for _ in range(6)]

    for l in range(depth):
        mods = ada_modulation(c_all, w_ada[l], b_ada[l])
        w_in_l = w_in[l]
        seg = [0, d_sb, 2 * d_sb, 3 * d_sb, 3 * d_sb + d_hk, 3 * d_sb + 2 * d_hk,
               3 * d_sb + 2 * d_hk + d_hg, 3 * d_sb + 2 * d_hk + 2 * d_hg]
        w_seg = [w_in_l[:, seg[i]:seg[i + 1]].astype(BF16) for i in range(7)]
        wo_a = w_out[l, :d_sb].astype(BF16)
        wo_b = w_out[l, d_sb:].astype(BF16)
        gain_a = norm_a[l].reshape(-1, 1, HEAD_DIM)
        gain_b = norm_b[l].reshape(-1, 1, HEAD_DIM)
        lb = lower_bounds[l].reshape(hg_heads, 1, HEAD_DIM)
        x1s, h2s, lgs, gate2s = [], [], [], []
        row0 = 0
        for si, x in enumerate(streams):
            b = x.shape[0]
            m = mods[row0:row0 + b].reshape(b, N_MOD, 1, d)
            row0 += b
            shift1, scale1, gate1, shift2, scale2, gate2 = [m[:, i] for i in range(N_MOD)]
            h = modulate(x, scale1, shift1)
            (q_a,) = project_heads(h, w_seg[0], [BF16], scale=sb_scale)
            k_f, k_h = project_heads(h, w_seg[1], [F32, BF16])
            v_f, v_h = project_heads(h, w_seg[2], [F32, BF16])
            (q_b,) = project_heads(h, w_seg[3], [F32])
            (f_b,) = project_heads(h, w_seg[4], [F32])
            (i_b,) = project_heads(h, w_seg[5], [F32])
            (g_b,) = project_heads(h, w_seg[6], [F32])
            if si == 0:
                o_a = stick_breaking(q_a, k_h, v_h, gain_a)
                s0 = jnp.zeros((b, hg_heads, d_hk // hg_heads, HEAD_DIM), F32)
            else:
                o_a = stick_breaking(q_a, k_h, v_h, gain_a, cache_sb_k[l], cache_sb_v[l])
                s0 = state_hgrn[l]
            o_b, s_new = hgrn2(q_b, f_b, i_b, g_b, lb, s0, gain_b)
            x1, h2, lg = out_projection(o_a, o_b, wo_a, wo_b, x, gate1, scale2, shift2,
                                        ln1_g[l], ln1_b[l], router_w[l], router_b[l], alpha)
            x1s.append(x1)
            h2s.append(h2.reshape(-1, d))
            lgs.append(lg.reshape(-1, n_exp))
            gate2s.append(gate2)
            outs[3 * si + 0].append(k_f)
            outs[3 * si + 1].append(v_f)
            outs[3 * si + 2].append(s_new)

        slot_tok, slot_gate, block_expert, n_used, pos = _route(jnp.concatenate(lgs, axis=0))
        xs = gather_rows(jnp.concatenate(h2s, axis=0), slot_tok)
        ys = expert_ffn(xs, slot_gate, block_expert, n_used,
                        w_gate[l].astype(BF16), b_gate[l][:, None, :],
                        w_up[l].astype(BF16), b_up[l][:, None, :],
                        w_down[l].astype(BF16), b_down[l][:, None, :])
        tok0 = 0
        new_streams = []
        for si, x1 in enumerate(x1s):
            new_streams.append(combine(ys, pos, x1, gate2s[si], ln2_g[l], ln2_b[l], alpha, tok0))
            tok0 += x1.shape[0] * x1.shape[1]
        streams = new_streams

    stack = lambda xs_: jnp.stack(xs_)
    return (streams[0], streams[1], stack(outs[0]), stack(outs[1]), stack(outs[2]),
            stack(outs[3]), stack(outs[4]), stack(outs[5]))
```

```python
import functools

import jax
import jax.numpy as jnp
from jax import lax
from jax.experimental import pallas as pl
from jax.experimental.pallas import tpu as pltpu

HEAD_DIM = 128
TOP_K = 4
SWIGLU_LIMIT = 7.0
SWIGLU_ALPHA = 1.702
LN_EPS = 1e-5
RMS_EPS = 1e-6
N_MOD = 6

LANES = 128
SB_TILE = 256
SB_DEAD = -104.0
HG_GROUP = 16
HG_UNROLL = 8
HG_SAFE = 60.0
MOE_TM = 256
VMEM_LIMIT = 56 * 1024 * 1024

F32 = jnp.float32
BF16 = jnp.bfloat16


def _params(*sem):
    return pltpu.CompilerParams(dimension_semantics=sem, vmem_limit_bytes=VMEM_LIMIT)


def _dot(a, b):
    return jnp.dot(a, b, preferred_element_type=F32)


def _dot_nt(a, b):
    return lax.dot_general(a, b, (((1,), (1,)), ((), ())), preferred_element_type=F32)


def _dot_tn(a, b):
    return lax.dot_general(a, b, (((0,), (0,)), ((), ())), preferred_element_type=F32)


def _split3(x):
    hi = x.astype(BF16)
    r1 = x - hi.astype(F32)
    mid = r1.astype(BF16)
    lo = (r1 - mid.astype(F32)).astype(BF16)
    return hi, mid, lo


def _dot_exact_rhs(x, m):
    hi, mid, lo = _split3(x)
    return _dot(m, hi) + _dot(m, mid) + _dot(m, lo)


def _softplus(z):
    return jnp.maximum(z, 0.0) + jnp.log(1.0 + jnp.exp(-jnp.abs(z)))


def _ada_kernel(c_ref, w_ref, b_ref, o_ref):
    c = c_ref[...]
    s = (c * jax.nn.sigmoid(c)).astype(BF16)
    o_ref[...] = _dot(s, w_ref[...].astype(BF16)) + b_ref[...]


def ada_modulation(c, w, b):
    r, d = c.shape
    n = w.shape[1]
    tn = min(n, 512)
    return pl.pallas_call(
        _ada_kernel,
        grid=(n // tn,),
        in_specs=[pl.BlockSpec((r, d), lambda j: (0, 0)),
                  pl.BlockSpec((d, tn), lambda j: (0, j)),
                  pl.BlockSpec((1, tn), lambda j: (0, j))],
        out_specs=pl.BlockSpec((r, tn), lambda j: (0, j)),
        out_shape=jax.ShapeDtypeStruct((r, n), F32),
        compiler_params=_params("arbitrary"),
        name="ada_modulation",
    )(c, w, b.reshape(1, n))


def _modulate_kernel(x_ref, sc_ref, sh_ref, o_ref):
    o_ref[...] = (x_ref[...] * (1.0 + sc_ref[...]) + sh_ref[...]).astype(o_ref.dtype)


def _row_tiles(b, t):
    tt = min(t, 512)
    bb = max(1, min(b, 256 // tt)) if tt < 256 else 1
    return bb, tt


def modulate(x, scale, shift):
    b, t, d = x.shape
    bb, tt = _row_tiles(b, t)
    mod_spec = pl.BlockSpec((bb, 1, d), lambda i, j: (i, 0, 0))
    return pl.pallas_call(
        _modulate_kernel,
        grid=(b // bb, t // tt),
        in_specs=[pl.BlockSpec((bb, tt, d), lambda i, j: (i, j, 0)), mod_spec, mod_spec],
        out_specs=pl.BlockSpec((bb, tt, d), lambda i, j: (i, j, 0)),
        out_shape=jax.ShapeDtypeStruct((b, t, d), BF16),
        compiler_params=_params("parallel", "parallel"),
        name="modulate",
    )(x, scale, shift)


def _proj_kernel(h_ref, w_ref, *o_refs, scale):
    bb, tt, d = h_ref.shape
    acc = _dot(h_ref[...].reshape(bb * tt, d), w_ref[...])
    if scale != 1.0:
        acc = acc * scale
    heads = o_refs[0].shape[1]
    for hh in range(heads):
        blk = acc[:, hh * HEAD_DIM:(hh + 1) * HEAD_DIM].reshape(bb, tt, HEAD_DIM)
        for o_ref in o_refs:
            o_ref[:, hh] = blk.astype(o_ref.dtype)


def project_heads(h, w, out_dtypes, scale=1.0):
    b, t, d = h.shape
    n = w.shape[1]
    heads = n // HEAD_DIM
    bb, tt = _row_tiles(b, t)
    tn = min(n, 1024)
    hpb = tn // HEAD_DIM
    out_spec = pl.BlockSpec((bb, hpb, tt, HEAD_DIM), lambda i, j, k: (i, k, j, 0))
    outs = pl.pallas_call(
        functools.partial(_proj_kernel, scale=scale),
        grid=(b // bb, t // tt, n // tn),
        in_specs=[pl.BlockSpec((bb, tt, d), lambda i, j, k: (i, j, 0)),
                  pl.BlockSpec((d, tn), lambda i, j, k: (0, k))],
        out_specs=[out_spec] * len(out_dtypes),
        out_shape=[jax.ShapeDtypeStruct((b, heads, t, HEAD_DIM), dt) for dt in out_dtypes],
        compiler_params=_params("parallel", "parallel", "arbitrary"),
        name="project_heads",
    )(h, w)
    return outs


def _sb_block(q, k, v, u_ext, carry, masked):
    tk = k.shape[0]
    z = _dot_nt(q, k)
    lf = -_softplus(z)
    if masked:
        row = lax.broadcasted_iota(jnp.int32, z.shape, 0)
        col = lax.broadcasted_iota(jnp.int32, z.shape, 1)
        earlier = col < row
        lf = jnp.where(earlier, lf, 0.0)
    hi = lf.astype(BF16)
    lo = (lf - hi.astype(F32)).astype(BF16)
    cs = _dot(hi, u_ext) + _dot(lo, u_ext)
    if tk >= LANES:
        later = jnp.concatenate([carry] * (tk // LANES), axis=1)
    else:
        later = carry[:, :tk]
    after = cs[:, LANES:] + later
    w = jnp.exp(z + lf + after)
    if masked:
        w = jnp.where(earlier, w, 0.0)
    out = _dot(w.astype(BF16), v)
    return out, carry + cs[:, :LANES]


def _sb_kernel(q_ref, kd_ref, vd_ref, kp_ref, vp_ref, ud_ref, up_ref, gain_ref, o_ref,
               acc_ref, carry_ref, *, past_blocks):
    tq = q_ref.shape[2]
    tk = up_ref.shape[0]
    q = q_ref[0, 0]
    out, carry = _sb_block(q, kd_ref[0, 0].astype(BF16), vd_ref[0, 0].astype(BF16), ud_ref[...],
                           jnp.zeros((tq, LANES), F32), masked=True)
    acc_ref[...] = out
    carry_ref[...] = carry
    n_past = pl.program_id(2) * (tq // tk) if past_blocks is None else past_blocks

    def cond(state):
        kb, alive = state
        return jnp.logical_and(kb >= 0, alive)

    def body(state):
        kb, _ = state
        start = pl.multiple_of(kb * tk, tk)
        k = kp_ref[0, 0, pl.ds(start, tk), :].astype(BF16)
        v = vp_ref[0, 0, pl.ds(start, tk), :].astype(BF16)
        out, carry = _sb_block(q, k, v, up_ref[...], carry_ref[...], masked=False)
        acc_ref[...] += out
        carry_ref[...] = carry
        return kb - 1, jnp.max(carry) > SB_DEAD

    lax.while_loop(cond, body, (n_past - 1, jnp.max(carry) > SB_DEAD))
    o = acc_ref[...]
    o = o * lax.rsqrt(jnp.mean(o * o, axis=-1, keepdims=True) + RMS_EPS)
    o_ref[0] = (o * gain_ref[0]).astype(o_ref.dtype)


def _suffix_matrix(tk):
    j = jnp.arange(tk)[:, None]
    c = jnp.arange(LANES + tk)[None, :]
    return jnp.where((c < LANES) | (j > c - LANES), 1.0, 0.0).astype(BF16)


def stick_breaking(q, k, v, gain, past_k=None, past_v=None):
    b, h, t, d = q.shape
    if past_k is None:
        tq = min(t, SB_TILE)
        tk = tq
        kp, vp, past_blocks = k, v, None
    else:
        tq = t
        tk = min(past_k.shape[2], SB_TILE)
        kp, vp, past_blocks = past_k, past_v, past_k.shape[2] // tk
    p = kp.shape[2]
    tile_spec = pl.BlockSpec((1, 1, tq, d), lambda i, j, m: (i, j, m, 0))
    past_spec = pl.BlockSpec((1, 1, p, d), lambda i, j, m: (i, j, 0, 0))
    return pl.pallas_call(
        functools.partial(_sb_kernel, past_blocks=past_blocks),
        grid=(b, h, t // tq),
        in_specs=[tile_spec, tile_spec, tile_spec, past_spec, past_spec,
                  pl.BlockSpec((tq, tq + LANES), lambda i, j, m: (0, 0)),
                  pl.BlockSpec((tk, tk + LANES), lambda i, j, m: (0, 0)),
                  pl.BlockSpec((1, 1, d), lambda i, j, m: (j, 0, 0))],
        out_specs=pl.BlockSpec((1, tq, d), lambda i, j, m: (i, m, j)),
        out_shape=jax.ShapeDtypeStruct((b, t, h * d), BF16),
        scratch_shapes=[pltpu.VMEM((tq, LANES), F32), pltpu.VMEM((tq, LANES), F32)],
        compiler_params=_params("parallel", "parallel", "arbitrary"),
        name="stick_breaking",
    )(q, k, v, kp, vp, _suffix_matrix(tq), _suffix_matrix(tk), gain)


def _hgrn_kernel(q_ref, a_ref, i_ref, g_ref, lb_ref, s0_ref, gain_ref, cum_ref, tot_ref, ones_ref,
                 o_ref, s_ref, st_ref, qd_ref, kd_ref, vb_ref, dec_ref, od_ref, oi_ref):
    tt = q_ref.shape[2]
    ng = tt // HG_GROUP
    ti = pl.program_id(2)

    @pl.when(ti == 0)
    def _():
        st_ref[...] = s0_ref[0, 0].T

    lb = lb_ref[0]
    a = a_ref[0, 0]
    log_sig = -_softplus(-a)
    x = jnp.log(lb)
    y = jnp.log1p(-lb) + log_sig
    log_f = jnp.maximum(x, y) + jnp.log(1.0 + jnp.exp(-jnp.abs(x - y)))
    kk = (1.0 - lb) * jax.nn.sigmoid(-a)
    q = q_ref[0, 0]
    v = i_ref[0, 0]

    cum = cum_ref[...]
    tot = tot_ref[...]
    rows = cum.shape[0]
    b_parts, e_parts = [], []
    for r in range(tt // rows):
        blk = log_f[r * rows:(r + 1) * rows]
        b_parts.append(_dot_exact_rhs(blk, cum))
        e_parts.append(_dot_exact_rhs(blk, tot))
    bcum = jnp.concatenate(b_parts, axis=0) if len(b_parts) > 1 else b_parts[0]
    bend = jnp.concatenate(e_parts, axis=0) if len(e_parts) > 1 else e_parts[0]
    qd = q * jnp.exp(bcum)
    qd_ref[...] = qd.astype(BF16)
    kd_ref[...] = (kk * jnp.exp(bend - bcum)).astype(BF16)
    vb_ref[...] = v.astype(BF16)
    dec_ref[...] = jnp.exp(bend)

    safe = jnp.min(bend) > -HG_SAFE

    @pl.when(safe)
    def _():
        kf = kk * jnp.exp(-bcum)
        same_group_earlier = cum > 0
        for r in range(tt // rows):
            sl = slice(r * rows, (r + 1) * rows)
            sc = _dot_nt(qd[sl].astype(BF16), kf[sl].astype(BF16))
            sc = jnp.where(same_group_earlier, sc, 0.0)
            od_ref[sl, :] = _dot(sc.astype(BF16), v[sl].astype(BF16))

    @pl.when(jnp.logical_not(safe))
    def _():
        b3 = bcum.reshape(ng, HG_GROUP, HEAD_DIM)
        q3 = q.reshape(ng, HG_GROUP, HEAD_DIM)
        k3 = kk.reshape(ng, HG_GROUP, HEAD_DIM)
        v3 = v.reshape(ng, HG_GROUP, HEAD_DIM)
        pos = lax.broadcasted_iota(jnp.int32, (ng, HG_GROUP, HEAD_DIM), 1)
        ones = ones_ref[...]
        od = jnp.zeros((tt, HEAD_DIM), F32)
        for s in range(HG_GROUP):
            diff = b3 - b3[:, s:s + 1, :]
            wgt = jnp.where(pos >= s, jnp.exp(jnp.minimum(diff, 0.0)), 0.0)
            prod = (q3 * k3[:, s:s + 1, :] * wgt).reshape(tt, HEAD_DIM)
            score = _dot(prod.astype(BF16), ones)
            vs = jnp.broadcast_to(v3[:, s:s + 1, :], (ng, HG_GROUP, HEAD_DIM)).reshape(tt, HEAD_DIM)
            od = od + score * vs
        od_ref[...] = od

    def step(i, st):
        r0 = pl.multiple_of(i * HG_GROUP, HG_GROUP)
        rows_i = pl.ds(r0, HG_GROUP)
        oi_ref[rows_i, :] = _dot_nt(qd_ref[rows_i, :], st.astype(BF16))
        upd = _dot_tn(vb_ref[rows_i, :], kd_ref[rows_i, :])
        return st * dec_ref[pl.ds(r0, 1), :] + upd

    st_ref[...] = lax.fori_loop(0, ng, step, st_ref[...], unroll=min(ng, HG_UNROLL))

    o = od_ref[...] + oi_ref[...]
    o = o * lax.rsqrt(jnp.mean(o * o, axis=-1, keepdims=True) + RMS_EPS) * gain_ref[0]
    gg = g_ref[0, 0]
    o_ref[0] = (o * (gg * jax.nn.sigmoid(gg))).astype(o_ref.dtype)

    @pl.when(ti == pl.num_programs(2) - 1)
    def _():
        s_ref[0, 0] = st_ref[...].T


def _group_matrices(rows):
    r = jnp.arange(rows)
    same = (r[:, None] // HG_GROUP) == (r[None, :] // HG_GROUP)
    cum = jnp.where(same & (r[None, :] <= r[:, None]), 1.0, 0.0).astype(BF16)
    tot = jnp.where(same, 1.0, 0.0).astype(BF16)
    return cum, tot


def hgrn2(q, a, i, g, lb, s0, gain):
    b, h, t, d = q.shape
    tt = min(t, 1024)
    rows = min(tt, LANES)
    cum, tot = _group_matrices(rows)
    ones = jnp.ones((d, d), BF16)
    seq_spec = pl.BlockSpec((1, 1, tt, d), lambda bi, hi, ti: (bi, hi, ti, 0))
    head_spec = pl.BlockSpec((1, 1, d), lambda bi, hi, ti: (hi, 0, 0))
    state_spec = pl.BlockSpec((1, 1, d, d), lambda bi, hi, ti: (bi, hi, 0, 0))
    const = lambda shape: pl.BlockSpec(shape, lambda bi, hi, ti: (0, 0))
    return pl.pallas_call(
        _hgrn_kernel,
        grid=(b, h, t // tt),
        in_specs=[seq_spec, seq_spec, seq_spec, seq_spec, head_spec, state_spec, head_spec,
                  const((rows, rows)), const((rows, rows)), const((d, d))],
        out_specs=[pl.BlockSpec((1, tt, d), lambda bi, hi, ti: (bi, ti, hi)), state_spec],
        out_shape=[jax.ShapeDtypeStruct((b, t, h * d), BF16),
                   jax.ShapeDtypeStruct((b, h, d, d), F32)],
        scratch_shapes=([pltpu.VMEM((d, d), F32)] + [pltpu.VMEM((tt, d), BF16)] * 3
                        + [pltpu.VMEM((tt, d), F32)] * 3),
        compiler_params=_params("parallel", "parallel", "arbitrary"),
        name="hgrn2",
    )(q, a, i, g, lb, s0, gain, cum, tot, ones)


def _layer_norm(x, g, b):
    mu = jnp.mean(x, axis=-1, keepdims=True)
    xc = x - mu
    var = jnp.mean(xc * xc, axis=-1, keepdims=True)
    return xc * lax.rsqrt(var + LN_EPS) * g + b


def _out_kernel(oa_ref, ob_ref, wa_ref, wb_ref, x_ref, gate_ref, sc_ref, sh_ref, lng_ref, lnb_ref,
                rw_ref, rb_ref, x1_ref, h2_ref, lg_ref, acc_ref, *, alpha):
    bb, tt, dh = oa_ref.shape
    tn = wa_ref.shape[1]
    n = pl.program_id(2)
    part = (_dot(oa_ref[...].reshape(bb * tt, dh), wa_ref[...])
            + _dot(ob_ref[...].reshape(bb * tt, dh), wb_ref[...]))
    acc_ref[n] = part

    @pl.when(n == pl.num_programs(2) - 1)
    def _():
        d = x_ref.shape[2]
        mix = jnp.concatenate([acc_ref[k] for k in range(d // tn)], axis=1).reshape(bb, tt, d)
        x1 = _layer_norm(alpha * x_ref[...] + gate_ref[...] * mix, lng_ref[...], lnb_ref[...])
        x1_ref[...] = x1
        h2 = x1 * (1.0 + sc_ref[...]) + sh_ref[...]
        h2_ref[...] = h2.astype(h2_ref.dtype)
        hh, hm, hl = _split3(h2.reshape(bb * tt, d))
        w = rw_ref[...]
        wh = w.astype(BF16)
        wl = (w - wh.astype(F32)).astype(BF16)
        lg = (_dot(hh, wh) + (_dot(hm, wh) + _dot(hh, wl))) + (_dot(hl, wh) + _dot(hm, wl))
        lg_ref[...] = (lg + rb_ref[...]).reshape(bb, tt, lg.shape[-1])


def out_projection(oa, ob, wa, wb, x, gate, scale, shift, ln_g, ln_b, router_w, router_b, alpha):
    b, t, dh = oa.shape
    d = x.shape[2]
    e = router_w.shape[1]
    bb, tt = _row_tiles(b, t)
    tt = min(tt, 256)
    tn = min(d, 512)
    row = lambda w_: pl.BlockSpec((bb, tt, w_), lambda i, j, n: (i, j, 0))
    mod = pl.BlockSpec((bb, 1, d), lambda i, j, n: (i, 0, 0))
    vec = lambda w_: pl.BlockSpec((1, w_), lambda i, j, n: (0, 0))
    wspec = pl.BlockSpec((dh, tn), lambda i, j, n: (0, n))
    return pl.pallas_call(
        functools.partial(_out_kernel, alpha=alpha),
        grid=(b // bb, t // tt, d // tn),
        in_specs=[row(dh), row(dh), wspec, wspec, row(d), mod, mod, mod, vec(d), vec(d),
                  pl.BlockSpec((d, e), lambda i, j, n: (0, 0)), vec(e)],
        out_specs=[row(d), row(d), row(e)],
        out_shape=[jax.ShapeDtypeStruct((b, t, d), F32), jax.ShapeDtypeStruct((b, t, d), F32),
                   jax.ShapeDtypeStruct((b, t, e), F32)],
        scratch_shapes=[pltpu.VMEM((d // tn, bb * tt, tn), F32)],
        compiler_params=_params("parallel", "parallel", "arbitrary"),
        name="out_projection",
    )(oa, ob, wa, wb, x, gate, scale, shift, ln_g.reshape(1, d), ln_b.reshape(1, d),
      router_w, router_b.reshape(1, e))


DMA_UNROLL = 8


def _row_copy_loops(copy, n):
    def issue(blk, slot):
        def body(r, c):
            copy(blk, r, slot).start()
            return c
        lax.fori_loop(0, n, body, 0, unroll=DMA_UNROLL)

    def drain(blk, slot):
        def body(r, c):
            copy(blk, r, slot).wait()
            return c
        lax.fori_loop(0, n, body, 0, unroll=DMA_UNROLL)

    return issue, drain


def _gather_kernel(tok_ref, nb_ref, src_ref, o_ref, buf_ref, sem):
    i = pl.program_id(0)
    n_used = nb_ref[0]
    slot = i % 2

    def copy(blk, r, sl):
        return pltpu.make_async_copy(src_ref.at[pl.ds(tok_ref[blk * MOE_TM + r], 1), :],
                                     buf_ref.at[sl, pl.ds(r, 1), :], sem.at[sl])

    issue, drain = _row_copy_loops(copy, MOE_TM)

    @pl.when(jnp.logical_and(i == 0, n_used > 0))
    def _():
        issue(0, 0)

    @pl.when(i + 1 < n_used)
    def _():
        issue(i + 1, 1 - slot)

    @pl.when(i < n_used)
    def _():
        drain(i, slot)
        o_ref[...] = buf_ref[slot].astype(o_ref.dtype)

    @pl.when(i >= n_used)
    def _():
        o_ref[...] = jnp.zeros_like(o_ref)


def gather_rows(src, tok, n_used):
    r = tok.shape[0]
    d = src.shape[1]
    return pl.pallas_call(
        _gather_kernel,
        grid_spec=pltpu.PrefetchScalarGridSpec(
            num_scalar_prefetch=2, grid=(r // MOE_TM,),
            in_specs=[pl.BlockSpec(memory_space=pl.ANY)],
            out_specs=pl.BlockSpec((MOE_TM, d), lambda i, tok_, nb_: (i, 0)),
            scratch_shapes=[pltpu.VMEM((2, MOE_TM, d), F32), pltpu.SemaphoreType.DMA((2,))]),
        out_shape=jax.ShapeDtypeStruct((r, d), BF16),
        compiler_params=pltpu.CompilerParams(dimension_semantics=("arbitrary",),
                                             vmem_limit_bytes=VMEM_LIMIT, disable_bounds_checks=True),
        name="gather_rows",
    )(tok, n_used, src)


def _expert_changed(be_ref, i):
    return jnp.logical_or(i == 0, be_ref[i] != be_ref[jnp.maximum(i - 1, 0)])


def _gate_up_kernel(be_ref, nb_ref, x_ref, wg_ref, bg_ref, wu_ref, bu_ref, o_ref, wgb_ref, wub_ref):
    i = pl.program_id(1)
    live = i < nb_ref[0]

    @pl.when(jnp.logical_and(live, _expert_changed(be_ref, i)))
    def _():
        wgb_ref[...] = wg_ref[0].astype(BF16)
        wub_ref[...] = wu_ref[0].astype(BF16)

    @pl.when(live)
    def _():
        x = x_ref[...]
        gate = jnp.minimum(_dot(x, wgb_ref[...]) + bg_ref[0], SWIGLU_LIMIT)
        up = jnp.clip(_dot(x, wub_ref[...]) + bu_ref[0], -SWIGLU_LIMIT, SWIGLU_LIMIT)
        o_ref[...] = ((up + 1.0) * gate * jax.nn.sigmoid(SWIGLU_ALPHA * gate)).astype(o_ref.dtype)

    @pl.when(jnp.logical_not(live))
    def _():
        o_ref[...] = jnp.zeros_like(o_ref)


def _down_kernel(be_ref, nb_ref, a_ref, w_ref, b_ref, o_ref, wb_ref):
    i = pl.program_id(1)
    live = i < nb_ref[0]

    @pl.when(jnp.logical_and(live, _expert_changed(be_ref, i)))
    def _():
        wb_ref[...] = w_ref[0].astype(BF16)

    @pl.when(live)
    def _():
        o_ref[...] = _dot(a_ref[...], wb_ref[...]) + b_ref[0]

    @pl.when(jnp.logical_not(live))
    def _():
        o_ref[...] = jnp.zeros_like(o_ref)


def expert_ffn(xs, block_expert, n_used, layer, w_gate, b_gate, w_up, b_up, w_down, b_down):
    r, d = xs.shape
    f = w_gate.shape[3]
    nb = r // MOE_TM
    tn = min(f, 512)
    wspec = lambda k, n_: pl.BlockSpec((None, 1, k, n_), lambda j, i, be, nu: (layer, be[i], 0, j))
    bspec = lambda n_: pl.BlockSpec((1, 1, n_), lambda j, i, be, nu: (be[i], 0, j))
    act = pl.pallas_call(
        _gate_up_kernel,
        grid_spec=pltpu.PrefetchScalarGridSpec(
            num_scalar_prefetch=2, grid=(f // tn, nb),
            in_specs=[pl.BlockSpec((MOE_TM, d), lambda j, i, be, nu: (i, 0)),
                      wspec(d, tn), bspec(tn), wspec(d, tn), bspec(tn)],
            out_specs=pl.BlockSpec((MOE_TM, tn), lambda j, i, be, nu: (i, j)),
            scratch_shapes=[pltpu.VMEM((d, tn), BF16), pltpu.VMEM((d, tn), BF16)]),
        out_shape=jax.ShapeDtypeStruct((r, f), BF16),
        compiler_params=_params("arbitrary", "arbitrary"),
        name="expert_gate_up",
    )(block_expert, n_used, xs, w_gate, b_gate, w_up, b_up)
    tn = min(d, 1024)
    return pl.pallas_call(
        _down_kernel,
        grid_spec=pltpu.PrefetchScalarGridSpec(
            num_scalar_prefetch=2, grid=(d // tn, nb),
            in_specs=[pl.BlockSpec((MOE_TM, f), lambda j, i, be, nu: (i, 0)),
                      wspec(f, tn), bspec(tn)],
            out_specs=pl.BlockSpec((MOE_TM, tn), lambda j, i, be, nu: (i, j)),
            scratch_shapes=[pltpu.VMEM((f, tn), BF16)]),
        out_shape=jax.ShapeDtypeStruct((r, d), F32),
        compiler_params=_params("arbitrary", "arbitrary"),
        name="expert_down",
    )(block_expert, n_used, act, w_down, b_down)


COMBINE_TOKENS = 32


def _combine_kernel(pos_ref, ys_ref, x1_ref, rg_ref, gate_ref, lng_ref, lnb_ref, o_ref, buf_ref, sem,
                    *, alpha, tok_offset):
    _, tt, d = x1_ref.shape
    step = pl.program_id(0) * pl.num_programs(1) + pl.program_id(1)
    n_steps = pl.num_programs(0) * pl.num_programs(1)
    slot = step % 2

    def copy(stp, r, sl):
        t, k = r // TOP_K, r % TOP_K
        src_row = pos_ref[(tok_offset + stp * tt) * TOP_K + r]
        return pltpu.make_async_copy(ys_ref.at[pl.ds(src_row, 1), :],
                                     buf_ref.at[sl, k, pl.ds(t, 1), :], sem.at[sl])

    issue, drain = _row_copy_loops(copy, tt * TOP_K)

    @pl.when(step == 0)
    def _():
        issue(0, 0)

    @pl.when(step + 1 < n_steps)
    def _():
        issue(step + 1, 1 - slot)

    drain(step, slot)
    rg = rg_ref[0]
    ffn = ((rg[:, 0:1] * buf_ref[slot, 0] + rg[:, 1:2] * buf_ref[slot, 1])
           + (rg[:, 2:3] * buf_ref[slot, 2] + rg[:, 3:4] * buf_ref[slot, 3]))
    y = alpha * x1_ref[0] + gate_ref[0] * ffn
    o_ref[0] = _layer_norm(y, lng_ref[...], lnb_ref[...])


def combine(ys, pos, router_gates, x1, gate, ln_g, ln_b, alpha, tok_offset):
    b, t, d = x1.shape
    tt = min(t, COMBINE_TOKENS)
    row = lambda w_: pl.BlockSpec((1, tt, w_), lambda i, j, p: (i, j, 0))
    mod = pl.BlockSpec((1, 1, d), lambda i, j, p: (i, 0, 0))
    vec = pl.BlockSpec((1, d), lambda i, j, p: (0, 0))
    return pl.pallas_call(
        functools.partial(_combine_kernel, alpha=alpha, tok_offset=tok_offset),
        grid_spec=pltpu.PrefetchScalarGridSpec(
            num_scalar_prefetch=1, grid=(b, t // tt),
            in_specs=[pl.BlockSpec(memory_space=pl.ANY), row(d), row(TOP_K), mod, vec, vec],
            out_specs=row(d),
            scratch_shapes=[pltpu.VMEM((2, TOP_K, tt, d), F32), pltpu.SemaphoreType.DMA((2,))]),
        out_shape=jax.ShapeDtypeStruct((b, t, d), F32),
        compiler_params=pltpu.CompilerParams(dimension_semantics=("arbitrary", "arbitrary"),
                                             vmem_limit_bytes=VMEM_LIMIT, disable_bounds_checks=True),
        name="moe_combine",
    )(pos, ys, x1, router_gates, gate, ln_g.reshape(1, d), ln_b.reshape(1, d))


def _route(logits):
    n_tok, n_exp = logits.shape
    top_logit, top_e = lax.top_k(logits, TOP_K)
    gates = jax.nn.softmax(top_logit, axis=-1)
    nk = n_tok * TOP_K
    flat_e = top_e.reshape(nk).astype(jnp.int32)
    onehot = (flat_e[:, None] == jnp.arange(n_exp, dtype=jnp.int32)[None, :]).astype(jnp.int32)
    running = jnp.cumsum(onehot, axis=0)
    counts = running[-1]
    rank = jnp.sum(running * onehot, axis=1) - 1
    padded = (counts + MOE_TM - 1) // MOE_TM * MOE_TM
    padded_end = jnp.cumsum(padded)
    padded_start = padded_end - padded
    start = jnp.cumsum(counts) - counts
    pos = (padded_start[flat_e] + rank).astype(jnp.int32)
    n_blocks = -(-nk // MOE_TM) + n_exp
    block_start = jnp.arange(n_blocks, dtype=jnp.int32) * MOE_TM
    block_expert = jnp.minimum(jnp.sum(block_start[:, None] >= padded_end[None, :], axis=1),
                               n_exp - 1).astype(jnp.int32)
    n_used = (padded_end[-1:] // MOE_TM).astype(jnp.int32)
    order = jnp.argsort(flat_e).astype(jnp.int32)
    slot = jnp.arange(n_blocks * MOE_TM, dtype=jnp.int32)
    slot_e = jnp.repeat(block_expert, MOE_TM)
    within = slot - padded_start[slot_e]
    valid = within < counts[slot_e]
    src = order[jnp.clip(start[slot_e] + within, 0, nk - 1)]
    slot_tok = jnp.where(valid, src // TOP_K, 0).astype(jnp.int32)
    return slot_tok, gates, block_expert, n_used, pos


def kernel(x_prompt, x_sample, c_prompt, c_sample, cache_sb_k, cache_sb_v, state_hgrn,
           w_ada, b_ada, w_in, w_out, norm_a, norm_b, lb_logits,
           ln1_g, ln1_b, ln2_g, ln2_b, router_w, router_b,
           w_gate, b_gate, w_up, b_up, w_down, b_down):
    depth = w_ada.shape[0]
    d = x_prompt.shape[-1]
    alpha = (2.0 * depth) ** 0.25
    d_sb = cache_sb_k.shape[2] * HEAD_DIM
    hg_heads = state_hgrn.shape[2]
    d_hk = hg_heads * state_hgrn.shape[3]
    d_hg = hg_heads * state_hgrn.shape[4]
    sb_scale = HEAD_DIM ** -0.5
    n_exp = router_w.shape[-1]
    lower_bounds = jnp.cumsum(jax.nn.softmax(lb_logits.astype(F32), axis=0), axis=0)
    streams = [x_prompt, x_sample]
    conds = [c_prompt, c_sample]
    n_rows = [c.shape[0] for c in conds]
    pad = (-sum(n_rows)) % 8
    c_all = jnp.concatenate(conds + [jnp.zeros((pad, d), F32)], axis=0)
    outs = [[] for _ in range(6)]

    for l in range(depth):
        mods = ada_modulation(c_all, w_ada[l], b_ada[l])
        w_in_l = w_in[l]
        seg = [0, d_sb, 2 * d_sb, 3 * d_sb, 3 * d_sb + d_hk, 3 * d_sb + 2 * d_hk,
               3 * d_sb + 2 * d_hk + d_hg, 3 * d_sb + 2 * d_hk + 2 * d_hg]
        w_seg = [w_in_l[:, seg[i]:seg[i + 1]].astype(BF16) for i in range(7)]
        wo_a = w_out[l, :d_sb].astype(BF16)
        wo_b = w_out[l, d_sb:].astype(BF16)
        gain_a = norm_a[l].reshape(-1, 1, HEAD_DIM)
        gain_b = norm_b[l].reshape(-1, 1, HEAD_DIM)
        lb = lower_bounds[l].reshape(hg_heads, 1, HEAD_DIM)
        x1s, h2s, lgs, gate2s = [], [], [], []
        row0 = 0
        for si, x in enumerate(streams):
            b = x.shape[0]
            m = mods[row0:row0 + b].reshape(b, N_MOD, 1, d)
            row0 += b
            shift1, scale1, gate1, shift2, scale2, gate2 = [m[:, i] for i in range(N_MOD)]
            h = modulate(x, scale1, shift1)
            (q_a,) = project_heads(h, w_seg[0], [BF16], scale=sb_scale)
            k_f, k_h = project_heads(h, w_seg[1], [F32, BF16])
            v_f, v_h = project_heads(h, w_seg[2], [F32, BF16])
            (q_b,) = project_heads(h, w_seg[3], [F32])
            (f_b,) = project_heads(h, w_seg[4], [F32])
            (i_b,) = project_heads(h, w_seg[5], [F32])
            (g_b,) = project_heads(h, w_seg[6], [F32])
            if si == 0:
                o_a = stick_breaking(q_a, k_h, v_h, gain_a)
                s0 = jnp.zeros((b, hg_heads, d_hk // hg_heads, HEAD_DIM), F32)
            else:
                o_a = stick_breaking(q_a, k_h, v_h, gain_a, cache_sb_k[l], cache_sb_v[l])
                s0 = state_hgrn[l]
            o_b, s_new = hgrn2(q_b, f_b, i_b, g_b, lb, s0, gain_b)
            x1, h2, lg = out_projection(o_a, o_b, wo_a, wo_b, x, gate1, scale2, shift2,
                                        ln1_g[l], ln1_b[l], router_w[l], router_b[l], alpha)
            x1s.append(x1)
            h2s.append(h2.reshape(-1, d))
            lgs.append(lg.reshape(-1, n_exp))
            gate2s.append(gate2)
            outs[3 * si + 0].append(k_f)
            outs[3 * si + 1].append(v_f)
            outs[3 * si + 2].append(s_new)

        slot_tok, router_gates, block_expert, n_used, pos = _route(jnp.concatenate(lgs, axis=0))
        xs = gather_rows(jnp.concatenate(h2s, axis=0), slot_tok, n_used)
        ys = expert_ffn(xs, block_expert, n_used, l,
                        w_gate, b_gate[l][:, None, :], w_up, b_up[l][:, None, :],
                        w_down, b_down[l][:, None, :])
        tok0 = 0
        new_streams = []
        for si, x1 in enumerate(x1s):
            n_s = x1.shape[0] * x1.shape[1]
            rg = router_gates[tok0:tok0 + n_s].reshape(x1.shape[0], x1.shape[1], TOP_K)
            new_streams.append(combine(ys, pos, rg, x1, gate2s[si], ln2_g[l], ln2_b[l], alpha, tok0))
            tok0 += n_s
        streams = new_streams

    stack = lambda xs_: jnp.stack(xs_)
    return (streams[0], streams[1], stack(outs[0]), stack(outs[1]), stack(outs[2]),
            stack(outs[3]), stack(outs[4]), stack(outs[5]))
```

```python
import functools

import jax
import jax.numpy as jnp
from jax import lax
from jax.experimental import pallas as pl
from jax.experimental.pallas import tpu as pltpu

HEAD_DIM = 128
TOP_K = 4
SWIGLU_LIMIT = 7.0
SWIGLU_ALPHA = 1.702
LN_EPS = 1e-5
RMS_EPS = 1e-6
N_MOD = 6

LANES = 128
SB_TILE = 256
SB_SPLIT = 1
SB_DEAD = -104.0
HG_GROUP = 16
HG_UNROLL = 8
HG_CHUNK = 64
HG_CHUNK_UNROLL = 4
HG_SAFE = 60.0
MOE_TM = 256
VMEM_LIMIT = 56 * 1024 * 1024

F32 = jnp.float32
BF16 = jnp.bfloat16


def _params(*sem):
    return pltpu.CompilerParams(dimension_semantics=sem, vmem_limit_bytes=VMEM_LIMIT)


def _dot(a, b):
    return jnp.dot(a, b, preferred_element_type=F32)


def _dot_nt(a, b):
    return lax.dot_general(a, b, (((1,), (1,)), ((), ())), preferred_element_type=F32)


def _dot_tn(a, b):
    return lax.dot_general(a, b, (((0,), (0,)), ((), ())), preferred_element_type=F32)


def _split3(x):
    hi = x.astype(BF16)
    r1 = x - hi.astype(F32)
    mid = r1.astype(BF16)
    lo = (r1 - mid.astype(F32)).astype(BF16)
    return hi, mid, lo


def _dot_exact_rhs(x, m):
    hi, mid, lo = _split3(x)
    return _dot(m, hi) + _dot(m, mid) + _dot(m, lo)


def _softplus(z):
    return jnp.maximum(z, 0.0) + jnp.log(1.0 + jnp.exp(-jnp.abs(z)))


def _ada_kernel(c_ref, w_ref, b_ref, o_ref):
    c = c_ref[...]
    s = (c * jax.nn.sigmoid(c)).astype(BF16)
    o_ref[...] = _dot(s, w_ref[...].astype(BF16)) + b_ref[...]


def ada_modulation(c, w, b):
    r, d = c.shape
    n = w.shape[1]
    tn = min(n, 512)
    return pl.pallas_call(
        _ada_kernel,
        grid=(n // tn,),
        in_specs=[pl.BlockSpec((r, d), lambda j: (0, 0)),
                  pl.BlockSpec((d, tn), lambda j: (0, j)),
                  pl.BlockSpec((1, tn), lambda j: (0, j))],
        out_specs=pl.BlockSpec((r, tn), lambda j: (0, j)),
        out_shape=jax.ShapeDtypeStruct((r, n), F32),
        compiler_params=_params("arbitrary"),
        name="ada_modulation",
    )(c, w, b.reshape(1, n))


def _modulate_kernel(x_ref, sc_ref, sh_ref, o_ref):
    o_ref[...] = (x_ref[...] * (1.0 + sc_ref[...]) + sh_ref[...]).astype(o_ref.dtype)


def _row_tiles(b, t):
    tt = min(t, 512)
    bb = max(1, min(b, 256 // tt)) if tt < 256 else 1
    return bb, tt


def modulate(x, scale, shift):
    b, t, d = x.shape
    bb, tt = _row_tiles(b, t)
    mod_spec = pl.BlockSpec((bb, 1, d), lambda i, j: (i, 0, 0))
    return pl.pallas_call(
        _modulate_kernel,
        grid=(b // bb, t // tt),
        in_specs=[pl.BlockSpec((bb, tt, d), lambda i, j: (i, j, 0)), mod_spec, mod_spec],
        out_specs=pl.BlockSpec((bb, tt, d), lambda i, j: (i, j, 0)),
        out_shape=jax.ShapeDtypeStruct((b, t, d), BF16),
        compiler_params=_params("parallel", "parallel"),
        name="modulate",
    )(x, scale, shift)


def _proj_kernel(h_ref, w_ref, *o_refs, scale):
    bb, tt, d = h_ref.shape
    acc = _dot(h_ref[...].reshape(bb * tt, d), w_ref[...])
    if scale != 1.0:
        acc = acc * scale
    heads = o_refs[0].shape[1]
    for hh in range(heads):
        blk = acc[:, hh * HEAD_DIM:(hh + 1) * HEAD_DIM].reshape(bb, tt, HEAD_DIM)
        for o_ref in o_refs:
            o_ref[:, hh] = blk.astype(o_ref.dtype)


def project_heads(h, w, out_dtypes, scale=1.0):
    b, t, d = h.shape
    n = w.shape[1]
    heads = n // HEAD_DIM
    bb, tt = _row_tiles(b, t)
    tn = min(n, 1024)
    hpb = tn // HEAD_DIM
    out_spec = pl.BlockSpec((bb, hpb, tt, HEAD_DIM), lambda i, j, k: (i, k, j, 0))
    outs = pl.pallas_call(
        functools.partial(_proj_kernel, scale=scale),
        grid=(b // bb, t // tt, n // tn),
        in_specs=[pl.BlockSpec((bb, tt, d), lambda i, j, k: (i, j, 0)),
                  pl.BlockSpec((d, tn), lambda i, j, k: (0, k))],
        out_specs=[out_spec] * len(out_dtypes),
        out_shape=[jax.ShapeDtypeStruct((b, heads, t, HEAD_DIM), dt) for dt in out_dtypes],
        compiler_params=_params("parallel", "parallel", "arbitrary"),
        name="project_heads",
    )(h, w)
    return outs


def _sb_block(q, k, v, u_ext, carry, row0):
    tk = k.shape[0]
    masked = row0 is not None
    z = _dot_nt(q, k)
    lf = -_softplus(z)
    if masked:
        row = lax.broadcasted_iota(jnp.int32, z.shape, 0) + row0
        col = lax.broadcasted_iota(jnp.int32, z.shape, 1)
        earlier = col < row
        lf = jnp.where(earlier, lf, 0.0)
    hi = lf.astype(BF16)
    lo = (lf - hi.astype(F32)).astype(BF16)
    cs = _dot(hi, u_ext) + _dot(lo, u_ext)
    if tk >= LANES:
        later = jnp.concatenate([carry] * (tk // LANES), axis=1)
    else:
        later = carry[:, :tk]
    after = cs[:, LANES:] + later
    w = jnp.exp(z + lf + after)
    if masked:
        w = jnp.where(earlier, w, 0.0)
    out = _dot(w.astype(BF16), v)
    return out, carry + cs[:, :LANES]


def _sb_kernel(q_ref, kd_ref, vd_ref, kp_ref, vp_ref, ud_ref, up_ref, gain_ref, o_ref,
               acc_ref, carry_ref, *, past_blocks):
    tq = q_ref.shape[2]
    tk = up_ref.shape[0]
    n_split = SB_SPLIT if tq >= SB_TILE else 1
    rh = tq // n_split
    halves = [slice(h * rh, (h + 1) * rh) for h in range(n_split)]
    q = q_ref[0, 0]
    kd = kd_ref[0, 0].astype(BF16)
    vd = vd_ref[0, 0].astype(BF16)
    alive = False
    for h, rows in enumerate(halves):
        nk = (h + 1) * rh
        out, carry = _sb_block(q[rows], kd[:nk], vd[:nk], ud_ref[:nk, :LANES + nk],
                               jnp.zeros((rh, LANES), F32), row0=h * rh)
        acc_ref[rows, :] = out
        carry_ref[rows, :] = carry
        alive = jnp.logical_or(alive, jnp.max(carry) > SB_DEAD)
    n_past = pl.program_id(2) * (tq // tk) if past_blocks is None else past_blocks

    def cond(state):
        kb, alive = state
        return jnp.logical_and(kb >= 0, alive)

    def body(state):
        kb, _ = state
        start = pl.multiple_of(kb * tk, tk)
        k = kp_ref[0, 0, pl.ds(start, tk), :].astype(BF16)
        v = vp_ref[0, 0, pl.ds(start, tk), :].astype(BF16)
        alive = False
        for rows in halves:
            out, carry = _sb_block(q[rows], k, v, up_ref[...], carry_ref[rows, :], row0=None)
            acc_ref[rows, :] += out
            carry_ref[rows, :] = carry
            alive = jnp.logical_or(alive, jnp.max(carry) > SB_DEAD)
        return kb - 1, alive

    lax.while_loop(cond, body, (n_past - 1, alive))
    o = acc_ref[...]
    o = o * lax.rsqrt(jnp.mean(o * o, axis=-1, keepdims=True) + RMS_EPS)
    o_ref[0] = (o * gain_ref[0]).astype(o_ref.dtype)


def _suffix_matrix(tk):
    j = jnp.arange(tk)[:, None]
    c = jnp.arange(LANES + tk)[None, :]
    return jnp.where((c < LANES) | (j > c - LANES), 1.0, 0.0).astype(BF16)


def stick_breaking(q, k, v, gain, past_k=None, past_v=None):
    b, h, t, d = q.shape
    if past_k is None:
        tq = min(t, SB_TILE)
        tk = tq
        kp, vp, past_blocks = k, v, None
    else:
        tq = t
        tk = min(past_k.shape[2], SB_TILE)
        kp, vp, past_blocks = past_k, past_v, past_k.shape[2] // tk
    p = kp.shape[2]
    tile_spec = pl.BlockSpec((1, 1, tq, d), lambda i, j, m: (i, j, m, 0))
    past_spec = pl.BlockSpec((1, 1, p, d), lambda i, j, m: (i, j, 0, 0))
    return pl.pallas_call(
        functools.partial(_sb_kernel, past_blocks=past_blocks),
        grid=(b, h, t // tq),
        in_specs=[tile_spec, tile_spec, tile_spec, past_spec, past_spec,
                  pl.BlockSpec((tq, tq + LANES), lambda i, j, m: (0, 0)),
                  pl.BlockSpec((tk, tk + LANES), lambda i, j, m: (0, 0)),
                  pl.BlockSpec((1, 1, d), lambda i, j, m: (j, 0, 0))],
        out_specs=pl.BlockSpec((1, tq, d), lambda i, j, m: (i, m, j)),
        out_shape=jax.ShapeDtypeStruct((b, t, h * d), BF16),
        scratch_shapes=[pltpu.VMEM((tq, LANES), F32), pltpu.VMEM((tq, LANES), F32)],
        compiler_params=_params("parallel", "parallel", "arbitrary"),
        name="stick_breaking",
    )(q, k, v, kp, vp, _suffix_matrix(tq), _suffix_matrix(tk), gain)


def _blockwise_prefix(x, m):
    rows = m.shape[0]
    parts = [_dot_exact_rhs(x[r * rows:(r + 1) * rows], m) for r in range(x.shape[0] // rows)]
    return jnp.concatenate(parts, axis=0) if len(parts) > 1 else parts[0]


def _hgrn_kernel(q_ref, a_ref, i_ref, g_ref, lb_ref, s0_ref, gain_ref, cum_ref, tot_ref, ones_ref,
                 ccum_ref, ctot_ref, chalf_ref,
                 o_ref, s_ref, st_ref, qd_ref, kd_ref, vb_ref, dec_ref, od_ref, oi_ref):
    tt = q_ref.shape[2]
    ti = pl.program_id(2)

    @pl.when(ti == 0)
    def _():
        st_ref[...] = s0_ref[0, 0].T

    lb = lb_ref[0]
    a = a_ref[0, 0]
    log_sig = -_softplus(-a)
    x = jnp.log(lb)
    y = jnp.log1p(-lb) + log_sig
    log_f = jnp.maximum(x, y) + jnp.log(1.0 + jnp.exp(-jnp.abs(x - y)))
    kk = (1.0 - lb) * jax.nn.sigmoid(-a)
    q = q_ref[0, 0]
    v = i_ref[0, 0]

    vb_ref[...] = v.astype(BF16)

    b_half = _blockwise_prefix(log_f, chalf_ref[...])
    b_end = _blockwise_prefix(log_f, ctot_ref[...])
    safe = jnp.logical_and(jnp.min(b_half) > -HG_SAFE, jnp.min(b_end - b_half) > -HG_SAFE)

    def recurrence(step_rows, unroll):
        def step(i, st):
            r0 = pl.multiple_of(i * step_rows, step_rows)
            rows_i = pl.ds(r0, step_rows)
            oi_ref[rows_i, :] = _dot_nt(qd_ref[rows_i, :], st.astype(BF16))
            upd = _dot_tn(vb_ref[rows_i, :], kd_ref[rows_i, :])
            return st * dec_ref[pl.ds(r0, 1), :] + upd
        n = tt // step_rows
        st_ref[...] = lax.fori_loop(0, n, step, st_ref[...], unroll=min(n, unroll))

    @pl.when(safe)
    def _():
        ccum = ccum_ref[...]
        b_c = _blockwise_prefix(log_f, ccum)
        qd_ref[...] = (q * jnp.exp(b_c)).astype(BF16)
        kd_ref[...] = (kk * jnp.exp(b_end - b_c)).astype(BF16)
        dec_ref[...] = jnp.exp(b_end)
        qm = (q * jnp.exp(b_c - b_half)).astype(BF16)
        km = (kk * jnp.exp(b_half - b_c)).astype(BF16)
        same_chunk_earlier = ccum > 0
        rows = ccum.shape[0]
        for r in range(tt // rows):
            sl = slice(r * rows, (r + 1) * rows)
            sc = jnp.where(same_chunk_earlier, _dot_nt(qm[sl], km[sl]), 0.0)
            od_ref[sl, :] = _dot(sc.astype(BF16), vb_ref[sl, :])
        recurrence(min(tt, HG_CHUNK), HG_CHUNK_UNROLL)

    @pl.when(jnp.logical_not(safe))
    def _():
        ng = tt // HG_GROUP
        bcum = _blockwise_prefix(log_f, cum_ref[...])
        bend = _blockwise_prefix(log_f, tot_ref[...])
        qd_ref[...] = (q * jnp.exp(bcum)).astype(BF16)
        kd_ref[...] = (kk * jnp.exp(bend - bcum)).astype(BF16)
        dec_ref[...] = jnp.exp(bend)
        b3 = bcum.reshape(ng, HG_GROUP, HEAD_DIM)
        q3 = q.reshape(ng, HG_GROUP, HEAD_DIM)
        k3 = kk.reshape(ng, HG_GROUP, HEAD_DIM)
        v3 = v.reshape(ng, HG_GROUP, HEAD_DIM)
        pos = lax.broadcasted_iota(jnp.int32, (ng, HG_GROUP, HEAD_DIM), 1)
        ones = ones_ref[...]
        od = jnp.zeros((tt, HEAD_DIM), F32)
        for s in range(HG_GROUP):
            diff = b3 - b3[:, s:s + 1, :]
            wgt = jnp.where(pos >= s, jnp.exp(jnp.minimum(diff, 0.0)), 0.0)
            prod = (q3 * k3[:, s:s + 1, :] * wgt).reshape(tt, HEAD_DIM)
            score = _dot(prod.astype(BF16), ones)
            vs = jnp.broadcast_to(v3[:, s:s + 1, :], (ng, HG_GROUP, HEAD_DIM)).reshape(tt, HEAD_DIM)
            od = od + score * vs
        od_ref[...] = od
        recurrence(HG_GROUP, HG_UNROLL)

    o = od_ref[...] + oi_ref[...]
    o = o * lax.rsqrt(jnp.mean(o * o, axis=-1, keepdims=True) + RMS_EPS) * gain_ref[0]
    gg = g_ref[0, 0]
    o_ref[0] = (o * (gg * jax.nn.sigmoid(gg))).astype(o_ref.dtype)

    @pl.when(ti == pl.num_programs(2) - 1)
    def _():
        s_ref[0, 0] = st_ref[...].T


def _span_matrices(rows, span):
    r = jnp.arange(rows)
    same = (r[:, None] // span) == (r[None, :] // span)
    cum = jnp.where(same & (r[None, :] <= r[:, None]), 1.0, 0.0).astype(BF16)
    tot = jnp.where(same, 1.0, 0.0).astype(BF16)
    half = jnp.where(same & ((r[None, :] % span) < span // 2), 1.0, 0.0).astype(BF16)
    return cum, tot, half


def hgrn2(q, a, i, g, lb, s0, gain):
    b, h, t, d = q.shape
    tt = min(t, 1024)
    rows = min(tt, LANES)
    cum, tot, _ = _span_matrices(rows, HG_GROUP)
    ccum, ctot, chalf = _span_matrices(rows, min(tt, HG_CHUNK))
    ones = jnp.ones((d, d), BF16)
    seq_spec = pl.BlockSpec((1, 1, tt, d), lambda bi, hi, ti: (bi, hi, ti, 0))
    head_spec = pl.BlockSpec((1, 1, d), lambda bi, hi, ti: (hi, 0, 0))
    state_spec = pl.BlockSpec((1, 1, d, d), lambda bi, hi, ti: (bi, hi, 0, 0))
    const = lambda shape: pl.BlockSpec(shape, lambda bi, hi, ti: (0, 0))
    return pl.pallas_call(
        _hgrn_kernel,
        grid=(b, h, t // tt),
        in_specs=[seq_spec, seq_spec, seq_spec, seq_spec, head_spec, state_spec, head_spec,
                  const((rows, rows)), const((rows, rows)), const((d, d)),
                  const((rows, rows)), const((rows, rows)), const((rows, rows))],
        out_specs=[pl.BlockSpec((1, tt, d), lambda bi, hi, ti: (bi, ti, hi)), state_spec],
        out_shape=[jax.ShapeDtypeStruct((b, t, h * d), BF16),
                   jax.ShapeDtypeStruct((b, h, d, d), F32)],
        scratch_shapes=([pltpu.VMEM((d, d), F32)] + [pltpu.VMEM((tt, d), BF16)] * 3
                        + [pltpu.VMEM((tt, d), F32)] * 3),
        compiler_params=_params("parallel", "parallel", "arbitrary"),
        name="hgrn2",
    )(q, a, i, g, lb, s0, gain, cum, tot, ones, ccum, ctot, chalf)


def _layer_norm(x, g, b):
    mu = jnp.mean(x, axis=-1, keepdims=True)
    xc = x - mu
    var = jnp.mean(xc * xc, axis=-1, keepdims=True)
    return xc * lax.rsqrt(var + LN_EPS) * g + b


def _out_kernel(oa_ref, ob_ref, wa_ref, wb_ref, x_ref, gate_ref, sc_ref, sh_ref, lng_ref, lnb_ref,
                rw_ref, rb_ref, x1_ref, h2_ref, lg_ref, acc_ref, *, alpha):
    bb, tt, dh = oa_ref.shape
    tn = wa_ref.shape[1]
    n = pl.program_id(2)
    part = (_dot(oa_ref[...].reshape(bb * tt, dh), wa_ref[...])
            + _dot(ob_ref[...].reshape(bb * tt, dh), wb_ref[...]))
    acc_ref[n] = part

    @pl.when(n == pl.num_programs(2) - 1)
    def _():
        d = x_ref.shape[2]
        mix = jnp.concatenate([acc_ref[k] for k in range(d // tn)], axis=1).reshape(bb, tt, d)
        x1 = _layer_norm(alpha * x_ref[...] + gate_ref[...] * mix, lng_ref[...], lnb_ref[...])
        x1_ref[...] = x1
        h2 = x1 * (1.0 + sc_ref[...]) + sh_ref[...]
        h2_ref[...] = h2.astype(h2_ref.dtype)
        hh, hm, hl = _split3(h2.reshape(bb * tt, d))
        w = rw_ref[...]
        wh = w.astype(BF16)
        wl = (w - wh.astype(F32)).astype(BF16)
        lg = (_dot(hh, wh) + (_dot(hm, wh) + _dot(hh, wl))) + (_dot(hl, wh) + _dot(hm, wl))
        lg_ref[...] = (lg + rb_ref[...]).reshape(bb, tt, lg.shape[-1])


def out_projection(oa, ob, wa, wb, x, gate, scale, shift, ln_g, ln_b, router_w, router_b, alpha):
    b, t, dh = oa.shape
    d = x.shape[2]
    e = router_w.shape[1]
    bb, tt = _row_tiles(b, t)
    tt = min(tt, 256)
    tn = min(d, 512)
    row = lambda w_: pl.BlockSpec((bb, tt, w_), lambda i, j, n: (i, j, 0))
    mod = pl.BlockSpec((bb, 1, d), lambda i, j, n: (i, 0, 0))
    vec = lambda w_: pl.BlockSpec((1, w_), lambda i, j, n: (0, 0))
    wspec = pl.BlockSpec((dh, tn), lambda i, j, n: (0, n))
    return pl.pallas_call(
        functools.partial(_out_kernel, alpha=alpha),
        grid=(b // bb, t // tt, d // tn),
        in_specs=[row(dh), row(dh), wspec, wspec, row(d), mod, mod, mod, vec(d), vec(d),
                  pl.BlockSpec((d, e), lambda i, j, n: (0, 0)), vec(e)],
        out_specs=[row(d), row(d), row(e)],
        out_shape=[jax.ShapeDtypeStruct((b, t, d), F32), jax.ShapeDtypeStruct((b, t, d), F32),
                   jax.ShapeDtypeStruct((b, t, e), F32)],
        scratch_shapes=[pltpu.VMEM((d // tn, bb * tt, tn), F32)],
        compiler_params=_params("parallel", "parallel", "arbitrary"),
        name="out_projection",
    )(oa, ob, wa, wb, x, gate, scale, shift, ln_g.reshape(1, d), ln_b.reshape(1, d),
      router_w, router_b.reshape(1, e))


DMA_UNROLL = 8


def _row_copy_loops(copy, n):
    def issue(blk, slot):
        def body(r, c):
            copy(blk, r, slot).start()
            return c
        lax.fori_loop(0, n, body, 0, unroll=DMA_UNROLL)

    def drain(slot):
        def body(r, c):
            copy(0, 0, slot).wait()
            return c
        lax.fori_loop(0, n, body, 0, unroll=DMA_UNROLL)

    return issue, drain


def _gather_kernel(tok_ref, nb_ref, src_ref, o_ref, buf_ref, sem):
    i = pl.program_id(0)
    n_used = nb_ref[0]
    slot = i % 2

    def copy(blk, r, sl):
        return pltpu.make_async_copy(src_ref.at[pl.ds(tok_ref[blk * MOE_TM + r], 1), :],
                                     buf_ref.at[sl, pl.ds(r, 1), :], sem.at[sl])

    issue, drain = _row_copy_loops(copy, MOE_TM)

    @pl.when(jnp.logical_and(i == 0, n_used > 0))
    def _():
        issue(0, 0)

    @pl.when(i + 1 < n_used)
    def _():
        issue(i + 1, 1 - slot)

    @pl.when(i < n_used)
    def _():
        drain(slot)
        o_ref[...] = buf_ref[slot].astype(o_ref.dtype)

    @pl.when(i >= n_used)
    def _():
        o_ref[...] = jnp.zeros_like(o_ref)


def gather_rows(src, tok, n_used):
    r = tok.shape[0]
    d = src.shape[1]
    return pl.pallas_call(
        _gather_kernel,
        grid_spec=pltpu.PrefetchScalarGridSpec(
            num_scalar_prefetch=2, grid=(r // MOE_TM,),
            in_specs=[pl.BlockSpec(memory_space=pl.ANY)],
            out_specs=pl.BlockSpec((MOE_TM, d), lambda i, tok_, nb_: (i, 0)),
            scratch_shapes=[pltpu.VMEM((2, MOE_TM, d), F32), pltpu.SemaphoreType.DMA((2,))]),
        out_shape=jax.ShapeDtypeStruct((r, d), BF16),
        compiler_params=pltpu.CompilerParams(dimension_semantics=("arbitrary",),
                                             vmem_limit_bytes=VMEM_LIMIT, disable_bounds_checks=True),
        name="gather_rows",
    )(tok, n_used, src)


def _expert_changed(be_ref, i):
    return jnp.logical_or(i == 0, be_ref[i] != be_ref[jnp.maximum(i - 1, 0)])


def _stream_expert_weights(be_ref, nb_ref, nx_ref, w_hbms, wf_ref, wb_refs, sem, layer):
    j, i = pl.program_id(0), pl.program_id(1)
    tn = wf_ref.shape[2]
    live = i < nb_ref[0]

    def copies(e):
        cols = pl.ds(pl.multiple_of(j * tn, tn), tn)
        return [pltpu.make_async_copy(w.at[layer, e, :, cols], wf_ref.at[m], sem.at[m])
                for m, w in enumerate(w_hbms)]

    @pl.when(jnp.logical_and(live, i == 0))
    def _():
        for c in copies(be_ref[0]):
            c.start()

    @pl.when(jnp.logical_and(live, _expert_changed(be_ref, i)))
    def _():
        for c in copies(be_ref[i]):
            c.wait()
        for m, wb_ref in enumerate(wb_refs):
            wb_ref[...] = wf_ref[m].astype(BF16)
        nxt = nx_ref[i]

        @pl.when(nxt >= 0)
        def _():
            for c in copies(nxt):
                c.start()

    return live


def _gate_up_kernel(be_ref, nb_ref, nx_ref, x_ref, wg_hbm, bg_ref, wu_hbm, bu_ref, o_ref,
                    wf_ref, wgb_ref, wub_ref, sem, *, layer):
    live = _stream_expert_weights(be_ref, nb_ref, nx_ref, (wg_hbm, wu_hbm), wf_ref,
                                  (wgb_ref, wub_ref), sem, layer)

    @pl.when(live)
    def _():
        x = x_ref[...]
        gate = jnp.minimum(_dot(x, wgb_ref[...]) + bg_ref[0], SWIGLU_LIMIT)
        up = jnp.clip(_dot(x, wub_ref[...]) + bu_ref[0], -SWIGLU_LIMIT, SWIGLU_LIMIT)
        o_ref[...] = ((up + 1.0) * gate * jax.nn.sigmoid(SWIGLU_ALPHA * gate)).astype(o_ref.dtype)

    @pl.when(jnp.logical_not(live))
    def _():
        o_ref[...] = jnp.zeros_like(o_ref)


def _down_kernel(be_ref, nb_ref, nx_ref, a_ref, w_hbm, b_ref, o_ref, wf_ref, wb_ref, sem, *, layer):
    live = _stream_expert_weights(be_ref, nb_ref, nx_ref, (w_hbm,), wf_ref, (wb_ref,), sem, layer)

    @pl.when(live)
    def _():
        o_ref[...] = _dot(a_ref[...], wb_ref[...]) + b_ref[0]

    @pl.when(jnp.logical_not(live))
    def _():
        o_ref[...] = jnp.zeros_like(o_ref)


def expert_ffn(xs, block_expert, n_used, next_expert, layer, w_gate, b_gate, w_up, b_up, w_down, b_down):
    r, d = xs.shape
    f = w_gate.shape[3]
    nb = r // MOE_TM
    hbm = pl.BlockSpec(memory_space=pl.ANY)
    bspec = lambda n_: pl.BlockSpec((1, 1, n_), lambda j, i, be, nu, nx: (be[i], 0, j))
    rows = lambda w_: pl.BlockSpec((MOE_TM, w_), lambda j, i, be, nu, nx: (i, 0))
    tile = lambda n_: pl.BlockSpec((MOE_TM, n_), lambda j, i, be, nu, nx: (i, j))
    tn = min(f, 512)
    act = pl.pallas_call(
        functools.partial(_gate_up_kernel, layer=layer),
        grid_spec=pltpu.PrefetchScalarGridSpec(
            num_scalar_prefetch=3, grid=(f // tn, nb),
            in_specs=[rows(d), hbm, bspec(tn), hbm, bspec(tn)],
            out_specs=tile(tn),
            scratch_shapes=[pltpu.VMEM((2, d, tn), F32), pltpu.VMEM((d, tn), BF16),
                            pltpu.VMEM((d, tn), BF16), pltpu.SemaphoreType.DMA((2,))]),
        out_shape=jax.ShapeDtypeStruct((r, f), BF16),
        compiler_params=_params("arbitrary", "arbitrary"),
        name="expert_gate_up",
    )(block_expert, n_used, next_expert, xs, w_gate, b_gate, w_up, b_up)
    tn = min(d, 1024)
    return pl.pallas_call(
        functools.partial(_down_kernel, layer=layer),
        grid_spec=pltpu.PrefetchScalarGridSpec(
            num_scalar_prefetch=3, grid=(d // tn, nb),
            in_specs=[rows(f), hbm, bspec(tn)],
            out_specs=tile(tn),
            scratch_shapes=[pltpu.VMEM((1, f, tn), F32), pltpu.VMEM((f, tn), BF16),
                            pltpu.SemaphoreType.DMA((1,))]),
        out_shape=jax.ShapeDtypeStruct((r, d), F32),
        compiler_params=_params("arbitrary", "arbitrary"),
        name="expert_down",
    )(block_expert, n_used, next_expert, act, w_down, b_down)


COMBINE_TOKENS = 64


def _combine_kernel(pos_ref, ys_ref, x1_ref, rg_ref, gate_ref, lng_ref, lnb_ref, o_ref, buf_ref, sem,
                    *, alpha, tok_offset):
    _, tt, d = x1_ref.shape
    step = pl.program_id(0) * pl.num_programs(1) + pl.program_id(1)
    n_steps = pl.num_programs(0) * pl.num_programs(1)
    slot = step % 2

    def copy(stp, r, sl):
        t, k = r // TOP_K, r % TOP_K
        src_row = pos_ref[(tok_offset + stp * tt) * TOP_K + r]
        return pltpu.make_async_copy(ys_ref.at[pl.ds(src_row, 1), :],
                                     buf_ref.at[sl, k, pl.ds(t, 1), :], sem.at[sl])

    issue, drain = _row_copy_loops(copy, tt * TOP_K)

    @pl.when(step == 0)
    def _():
        issue(0, 0)

    @pl.when(step + 1 < n_steps)
    def _():
        issue(step + 1, 1 - slot)

    drain(slot)
    rg = rg_ref[0]
    ffn = ((rg[:, 0:1] * buf_ref[slot, 0] + rg[:, 1:2] * buf_ref[slot, 1])
           + (rg[:, 2:3] * buf_ref[slot, 2] + rg[:, 3:4] * buf_ref[slot, 3]))
    y = alpha * x1_ref[0] + gate_ref[0] * ffn
    o_ref[0] = _layer_norm(y, lng_ref[...], lnb_ref[...])


def combine(ys, pos, router_gates, x1, gate, ln_g, ln_b, alpha, tok_offset):
    b, t, d = x1.shape
    tt = min(t, COMBINE_TOKENS)
    row = lambda w_: pl.BlockSpec((1, tt, w_), lambda i, j, p: (i, j, 0))
    mod = pl.BlockSpec((1, 1, d), lambda i, j, p: (i, 0, 0))
    vec = pl.BlockSpec((1, d), lambda i, j, p: (0, 0))
    return pl.pallas_call(
        functools.partial(_combine_kernel, alpha=alpha, tok_offset=tok_offset),
        grid_spec=pltpu.PrefetchScalarGridSpec(
            num_scalar_prefetch=1, grid=(b, t // tt),
            in_specs=[pl.BlockSpec(memory_space=pl.ANY), row(d), row(TOP_K), mod, vec, vec],
            out_specs=row(d),
            scratch_shapes=[pltpu.VMEM((2, TOP_K, tt, d), F32), pltpu.SemaphoreType.DMA((2,))]),
        out_shape=jax.ShapeDtypeStruct((b, t, d), F32),
        compiler_params=pltpu.CompilerParams(dimension_semantics=("arbitrary", "arbitrary"),
                                             vmem_limit_bytes=VMEM_LIMIT, disable_bounds_checks=True),
        name="moe_combine",
    )(pos, ys, x1, router_gates, gate, ln_g.reshape(1, d), ln_b.reshape(1, d))


def _route(logits):
    n_tok, n_exp = logits.shape
    top_logit, top_e = lax.top_k(logits, TOP_K)
    gates = jax.nn.softmax(top_logit, axis=-1)
    nk = n_tok * TOP_K
    flat_e = top_e.reshape(nk).astype(jnp.int32)
    onehot = (flat_e[:, None] == jnp.arange(n_exp, dtype=jnp.int32)[None, :]).astype(jnp.int32)
    running = jnp.cumsum(onehot, axis=0)
    counts = running[-1]
    rank = jnp.sum(running * onehot, axis=1) - 1
    padded = (counts + MOE_TM - 1) // MOE_TM * MOE_TM
    padded_end = jnp.cumsum(padded)
    padded_start = padded_end - padded
    start = jnp.cumsum(counts) - counts
    pos = (padded_start[flat_e] + rank).astype(jnp.int32)
    n_blocks = -(-nk // MOE_TM) + n_exp
    block_start = jnp.arange(n_blocks, dtype=jnp.int32) * MOE_TM
    block_expert = jnp.minimum(jnp.sum(block_start[:, None] >= padded_end[None, :], axis=1),
                               n_exp - 1).astype(jnp.int32)
    n_used = (padded_end[-1:] // MOE_TM).astype(jnp.int32)
    blk = jnp.arange(n_blocks, dtype=jnp.int32)
    run_start = (blk < n_used[0]) & ((blk == 0) | (block_expert != jnp.roll(block_expert, 1)))
    later_start = lax.cummin(jnp.where(run_start, blk, n_blocks)[::-1])[::-1]
    next_start = jnp.concatenate([later_start[1:], jnp.full((1,), n_blocks, jnp.int32)])
    next_expert = jnp.where(next_start < n_blocks,
                            block_expert[jnp.minimum(next_start, n_blocks - 1)], -1).astype(jnp.int32)
    order = jnp.argsort(flat_e).astype(jnp.int32)
    slot = jnp.arange(n_blocks * MOE_TM, dtype=jnp.int32)
    slot_e = jnp.repeat(block_expert, MOE_TM)
    within = slot - padded_start[slot_e]
    valid = within < counts[slot_e]
    src = order[jnp.clip(start[slot_e] + within, 0, nk - 1)]
    slot_tok = jnp.where(valid, src // TOP_K, 0).astype(jnp.int32)
    return slot_tok, gates, block_expert, n_used, next_expert, pos


def kernel(x_prompt, x_sample, c_prompt, c_sample, cache_sb_k, cache_sb_v, state_hgrn,
           w_ada, b_ada, w_in, w_out, norm_a, norm_b, lb_logits,
           ln1_g, ln1_b, ln2_g, ln2_b, router_w, router_b,
           w_gate, b_gate, w_up, b_up, w_down, b_down):
    depth = w_ada.shape[0]
    d = x_prompt.shape[-1]
    alpha = (2.0 * depth) ** 0.25
    d_sb = cache_sb_k.shape[2] * HEAD_DIM
    hg_heads = state_hgrn.shape[2]
    d_hk = hg_heads * state_hgrn.shape[3]
    d_hg = hg_heads * state_hgrn.shape[4]
    sb_scale = HEAD_DIM ** -0.5
    n_exp = router_w.shape[-1]
    lower_bounds = jnp.cumsum(jax.nn.softmax(lb_logits.astype(F32), axis=0), axis=0)
    streams = [x_prompt, x_sample]
    conds = [c_prompt, c_sample]
    n_rows = [c.shape[0] for c in conds]
    pad = (-sum(n_rows)) % 8
    c_all = jnp.concatenate(conds + [jnp.zeros((pad, d), F32)], axis=0)
    outs = [[] for _ in range(6)]

    for l in range(depth):
        mods = ada_modulation(c_all, w_ada[l], b_ada[l])
        w_in_l = w_in[l]
        seg = [0, d_sb, 2 * d_sb, 3 * d_sb, 3 * d_sb + d_hk, 3 * d_sb + 2 * d_hk,
               3 * d_sb + 2 * d_hk + d_hg, 3 * d_sb + 2 * d_hk + 2 * d_hg]
        w_seg = [w_in_l[:, seg[i]:seg[i + 1]].astype(BF16) for i in range(7)]
        wo_a = w_out[l, :d_sb].astype(BF16)
        wo_b = w_out[l, d_sb:].astype(BF16)
        gain_a = norm_a[l].reshape(-1, 1, HEAD_DIM)
        gain_b = norm_b[l].reshape(-1, 1, HEAD_DIM)
        lb = lower_bounds[l].reshape(hg_heads, 1, HEAD_DIM)
        x1s, h2s, lgs, gate2s = [], [], [], []
        row0 = 0
        for si, x in enumerate(streams):
            b = x.shape[0]
            m = mods[row0:row0 + b].reshape(b, N_MOD, 1, d)
            row0 += b
            shift1, scale1, gate1, shift2, scale2, gate2 = [m[:, i] for i in range(N_MOD)]
            h = modulate(x, scale1, shift1)
            (q_a,) = project_heads(h, w_seg[0], [BF16], scale=sb_scale)
            k_f, k_h = project_heads(h, w_seg[1], [F32, BF16])
            v_f, v_h = project_heads(h, w_seg[2], [F32, BF16])
            (q_b,) = project_heads(h, w_seg[3], [F32])
            (f_b,) = project_heads(h, w_seg[4], [F32])
            (i_b,) = project_heads(h, w_seg[5], [F32])
            (g_b,) = project_heads(h, w_seg[6], [F32])
            if si == 0:
                o_a = stick_breaking(q_a, k_h, v_h, gain_a)
                s0 = jnp.zeros((b, hg_heads, d_hk // hg_heads, HEAD_DIM), F32)
            else:
                o_a = stick_breaking(q_a, k_h, v_h, gain_a, cache_sb_k[l], cache_sb_v[l])
                s0 = state_hgrn[l]
            o_b, s_new = hgrn2(q_b, f_b, i_b, g_b, lb, s0, gain_b)
            x1, h2, lg = out_projection(o_a, o_b, wo_a, wo_b, x, gate1, scale2, shift2,
                                        ln1_g[l], ln1_b[l], router_w[l], router_b[l], alpha)
            x1s.append(x1)
            h2s.append(h2.reshape(-1, d))
            lgs.append(lg.reshape(-1, n_exp))
            gate2s.append(gate2)
            outs[3 * si + 0].append(k_f)
            outs[3 * si + 1].append(v_f)
            outs[3 * si + 2].append(s_new)

        slot_tok, router_gates, block_expert, n_used, next_expert, pos = _route(
            jnp.concatenate(lgs, axis=0))
        xs = gather_rows(jnp.concatenate(h2s, axis=0), slot_tok, n_used)
        ys = expert_ffn(xs, block_expert, n_used, next_expert, l,
                        w_gate, b_gate[l][:, None, :], w_up, b_up[l][:, None, :],
                        w_down, b_down[l][:, None, :])
        tok0 = 0
        new_streams = []
        for si, x1 in enumerate(x1s):
            n_s = x1.shape[0] * x1.shape[1]
            rg = router_gates[tok0:tok0 + n_s].reshape(x1.shape[0], x1.shape[1], TOP_K)
            new_streams.append(combine(ys, pos, rg, x1, gate2s[si], ln2_g[l], ln2_b[l], alpha, tok0))
            tok0 += n_s
        streams = new_streams

    stack = lambda xs_: jnp.stack(xs_)
    return (streams[0], streams[1], stack(outs[0]), stack(outs[1]), stack(outs[2]),
            stack(outs[3]), stack(outs[4]), stack(outs[5]))
```

```python
import functools

import jax
import jax.numpy as jnp
from jax import lax
from jax.experimental import pallas as pl
from jax.experimental.pallas import tpu as pltpu

HEAD_DIM = 128
TOP_K = 4
SWIGLU_LIMIT = 7.0
SWIGLU_ALPHA = 1.702
LN_EPS = 1e-5
RMS_EPS = 1e-6
N_MOD = 6

LANES = 128
SB_TILE = 256
SB_SPLIT = 1
SB_DEAD = -104.0
HG_GROUP = 16
HG_UNROLL = 8
HG_CHUNK = 64
HG_CHUNK_UNROLL = 4
HG_SAFE = 60.0
MOE_TM = 1024
MOE_SUB = 256
VMEM_LIMIT = 56 * 1024 * 1024

F32 = jnp.float32
BF16 = jnp.bfloat16


def _params(*sem):
    return pltpu.CompilerParams(dimension_semantics=sem, vmem_limit_bytes=VMEM_LIMIT)


def _dot(a, b):
    return jnp.dot(a, b, preferred_element_type=F32)


def _dot_nt(a, b):
    return lax.dot_general(a, b, (((1,), (1,)), ((), ())), preferred_element_type=F32)


def _dot_tn(a, b):
    return lax.dot_general(a, b, (((0,), (0,)), ((), ())), preferred_element_type=F32)


def _split3(x):
    hi = x.astype(BF16)
    r1 = x - hi.astype(F32)
    mid = r1.astype(BF16)
    lo = (r1 - mid.astype(F32)).astype(BF16)
    return hi, mid, lo


def _dot_exact_rhs(x, m):
    hi, mid, lo = _split3(x)
    return _dot(m, hi) + _dot(m, mid) + _dot(m, lo)


def _softplus(z):
    return jnp.maximum(z, 0.0) + jnp.log(1.0 + jnp.exp(-jnp.abs(z)))


def _ada_kernel(c_ref, w_ref, b_ref, o_ref):
    c = c_ref[...]
    s = (c * jax.nn.sigmoid(c)).astype(BF16)
    o_ref[...] = _dot(s, w_ref[...].astype(BF16)) + b_ref[...]


def ada_modulation(c, w, b):
    r, d = c.shape
    n = w.shape[1]
    tn = min(n, 512)
    return pl.pallas_call(
        _ada_kernel,
        grid=(n // tn,),
        in_specs=[pl.BlockSpec((r, d), lambda j: (0, 0)),
                  pl.BlockSpec((d, tn), lambda j: (0, j)),
                  pl.BlockSpec((1, tn), lambda j: (0, j))],
        out_specs=pl.BlockSpec((r, tn), lambda j: (0, j)),
        out_shape=jax.ShapeDtypeStruct((r, n), F32),
        compiler_params=_params("arbitrary"),
        name="ada_modulation",
    )(c, w, b.reshape(1, n))


def _modulate_kernel(x_ref, sc_ref, sh_ref, o_ref):
    o_ref[...] = (x_ref[...] * (1.0 + sc_ref[...]) + sh_ref[...]).astype(o_ref.dtype)


def _row_tiles(b, t):
    tt = min(t, 512)
    bb = max(1, min(b, 256 // tt)) if tt < 256 else 1
    return bb, tt


def modulate(x, scale, shift):
    b, t, d = x.shape
    bb, tt = _row_tiles(b, t)
    mod_spec = pl.BlockSpec((bb, 1, d), lambda i, j: (i, 0, 0))
    return pl.pallas_call(
        _modulate_kernel,
        grid=(b // bb, t // tt),
        in_specs=[pl.BlockSpec((bb, tt, d), lambda i, j: (i, j, 0)), mod_spec, mod_spec],
        out_specs=pl.BlockSpec((bb, tt, d), lambda i, j: (i, j, 0)),
        out_shape=jax.ShapeDtypeStruct((b, t, d), BF16),
        compiler_params=_params("parallel", "parallel"),
        name="modulate",
    )(x, scale, shift)


def _proj_kernel(h_ref, w_ref, *o_refs, scale):
    bb, tt, d = h_ref.shape
    acc = _dot(h_ref[...].reshape(bb * tt, d), w_ref[...])
    if scale != 1.0:
        acc = acc * scale
    heads = o_refs[0].shape[1]
    for hh in range(heads):
        blk = acc[:, hh * HEAD_DIM:(hh + 1) * HEAD_DIM].reshape(bb, tt, HEAD_DIM)
        for o_ref in o_refs:
            o_ref[:, hh] = blk.astype(o_ref.dtype)


def project_heads(h, w, out_dtypes, scale=1.0):
    b, t, d = h.shape
    n = w.shape[1]
    heads = n // HEAD_DIM
    bb, tt = _row_tiles(b, t)
    tn = min(n, 1024)
    hpb = tn // HEAD_DIM
    out_spec = pl.BlockSpec((bb, hpb, tt, HEAD_DIM), lambda i, j, k: (i, k, j, 0))
    outs = pl.pallas_call(
        functools.partial(_proj_kernel, scale=scale),
        grid=(b // bb, t // tt, n // tn),
        in_specs=[pl.BlockSpec((bb, tt, d), lambda i, j, k: (i, j, 0)),
                  pl.BlockSpec((d, tn), lambda i, j, k: (0, k))],
        out_specs=[out_spec] * len(out_dtypes),
        out_shape=[jax.ShapeDtypeStruct((b, heads, t, HEAD_DIM), dt) for dt in out_dtypes],
        compiler_params=_params("parallel", "parallel", "arbitrary"),
        name="project_heads",
    )(h, w)
    return outs


def _sb_block(q, k, v, u_ext, carry, row0):
    tk = k.shape[0]
    masked = row0 is not None
    z = _dot_nt(q, k)
    lf = -_softplus(z)
    if masked:
        row = lax.broadcasted_iota(jnp.int32, z.shape, 0) + row0
        col = lax.broadcasted_iota(jnp.int32, z.shape, 1)
        earlier = col < row
        lf = jnp.where(earlier, lf, 0.0)
    hi = lf.astype(BF16)
    lo = (lf - hi.astype(F32)).astype(BF16)
    cs = _dot(hi, u_ext) + _dot(lo, u_ext)
    if tk >= LANES:
        later = jnp.concatenate([carry] * (tk // LANES), axis=1)
    else:
        later = carry[:, :tk]
    after = cs[:, LANES:] + later
    w = jnp.exp(z + lf + after)
    if masked:
        w = jnp.where(earlier, w, 0.0)
    out = _dot(w.astype(BF16), v)
    return out, carry + cs[:, :LANES]


def _sb_kernel(q_ref, kd_ref, vd_ref, kp_ref, vp_ref, ud_ref, up_ref, gain_ref, o_ref,
               acc_ref, carry_ref, *, past_blocks):
    tq = q_ref.shape[2]
    tk = up_ref.shape[0]
    n_split = SB_SPLIT if tq >= SB_TILE else 1
    rh = tq // n_split
    halves = [slice(h * rh, (h + 1) * rh) for h in range(n_split)]
    q = q_ref[0, 0]
    kd = kd_ref[0, 0].astype(BF16)
    vd = vd_ref[0, 0].astype(BF16)
    alive = False
    for h, rows in enumerate(halves):
        nk = (h + 1) * rh
        out, carry = _sb_block(q[rows], kd[:nk], vd[:nk], ud_ref[:nk, :LANES + nk],
                               jnp.zeros((rh, LANES), F32), row0=h * rh)
        acc_ref[rows, :] = out
        carry_ref[rows, :] = carry
        alive = jnp.logical_or(alive, jnp.max(carry) > SB_DEAD)
    n_past = pl.program_id(2) * (tq // tk) if past_blocks is None else past_blocks

    def cond(state):
        kb, alive = state
        return jnp.logical_and(kb >= 0, alive)

    def body(state):
        kb, _ = state
        start = pl.multiple_of(kb * tk, tk)
        k = kp_ref[0, 0, pl.ds(start, tk), :].astype(BF16)
        v = vp_ref[0, 0, pl.ds(start, tk), :].astype(BF16)
        alive = False
        for rows in halves:
            out, carry = _sb_block(q[rows], k, v, up_ref[...], carry_ref[rows, :], row0=None)
            acc_ref[rows, :] += out
            carry_ref[rows, :] = carry
            alive = jnp.logical_or(alive, jnp.max(carry) > SB_DEAD)
        return kb - 1, alive

    lax.while_loop(cond, body, (n_past - 1, alive))
    o = acc_ref[...]
    o = o * lax.rsqrt(jnp.mean(o * o, axis=-1, keepdims=True) + RMS_EPS)
    o_ref[0] = (o * gain_ref[0]).astype(o_ref.dtype)


def _suffix_matrix(tk):
    j = jnp.arange(tk)[:, None]
    c = jnp.arange(LANES + tk)[None, :]
    return jnp.where((c < LANES) | (j > c - LANES), 1.0, 0.0).astype(BF16)


def stick_breaking(q, k, v, gain, past_k=None, past_v=None):
    b, h, t, d = q.shape
    if past_k is None:
        tq = min(t, SB_TILE)
        tk = tq
        kp, vp, past_blocks = k, v, None
    else:
        tq = t
        tk = min(past_k.shape[2], SB_TILE)
        kp, vp, past_blocks = past_k, past_v, past_k.shape[2] // tk
    p = kp.shape[2]
    tile_spec = pl.BlockSpec((1, 1, tq, d), lambda i, j, m: (i, j, m, 0))
    past_spec = pl.BlockSpec((1, 1, p, d), lambda i, j, m: (i, j, 0, 0))
    return pl.pallas_call(
        functools.partial(_sb_kernel, past_blocks=past_blocks),
        grid=(b, h, t // tq),
        in_specs=[tile_spec, tile_spec, tile_spec, past_spec, past_spec,
                  pl.BlockSpec((tq, tq + LANES), lambda i, j, m: (0, 0)),
                  pl.BlockSpec((tk, tk + LANES), lambda i, j, m: (0, 0)),
                  pl.BlockSpec((1, 1, d), lambda i, j, m: (j, 0, 0))],
        out_specs=pl.BlockSpec((1, tq, d), lambda i, j, m: (i, m, j)),
        out_shape=jax.ShapeDtypeStruct((b, t, h * d), BF16),
        scratch_shapes=[pltpu.VMEM((tq, LANES), F32), pltpu.VMEM((tq, LANES), F32)],
        compiler_params=_params("parallel", "parallel", "arbitrary"),
        name="stick_breaking",
    )(q, k, v, kp, vp, _suffix_matrix(tq), _suffix_matrix(tk), gain)


def _blockwise_prefix(x, m):
    rows = m.shape[0]
    parts = [_dot_exact_rhs(x[r * rows:(r + 1) * rows], m) for r in range(x.shape[0] // rows)]
    return jnp.concatenate(parts, axis=0) if len(parts) > 1 else parts[0]


def _hgrn_kernel(q_ref, a_ref, i_ref, g_ref, lb_ref, s0_ref, gain_ref, cum_ref, tot_ref, ones_ref,
                 ccum_ref, ctot_ref, chalf_ref,
                 o_ref, s_ref, st_ref, qd_ref, kd_ref, vb_ref, dec_ref, od_ref, oi_ref):
    tt = q_ref.shape[2]
    ti = pl.program_id(2)

    @pl.when(ti == 0)
    def _():
        st_ref[...] = s0_ref[0, 0].T

    lb = lb_ref[0]
    a = a_ref[0, 0]
    log_sig = -_softplus(-a)
    x = jnp.log(lb)
    y = jnp.log1p(-lb) + log_sig
    log_f = jnp.maximum(x, y) + jnp.log(1.0 + jnp.exp(-jnp.abs(x - y)))
    kk = (1.0 - lb) * jax.nn.sigmoid(-a)
    q = q_ref[0, 0]
    v = i_ref[0, 0]

    vb_ref[...] = v.astype(BF16)

    b_half = _blockwise_prefix(log_f, chalf_ref[...])
    b_end = _blockwise_prefix(log_f, ctot_ref[...])
    safe = jnp.logical_and(jnp.min(b_half) > -HG_SAFE, jnp.min(b_end - b_half) > -HG_SAFE)

    def recurrence(step_rows, unroll):
        def step(i, st):
            r0 = pl.multiple_of(i * step_rows, step_rows)
            rows_i = pl.ds(r0, step_rows)
            oi_ref[rows_i, :] = _dot_nt(qd_ref[rows_i, :], st.astype(BF16))
            upd = _dot_tn(vb_ref[rows_i, :], kd_ref[rows_i, :])
            return st * dec_ref[pl.ds(r0, 1), :] + upd
        n = tt // step_rows
        st_ref[...] = lax.fori_loop(0, n, step, st_ref[...], unroll=min(n, unroll))

    @pl.when(safe)
    def _():
        ccum = ccum_ref[...]
        b_c = _blockwise_prefix(log_f, ccum)
        qd_ref[...] = (q * jnp.exp(b_c)).astype(BF16)
        kd_ref[...] = (kk * jnp.exp(b_end - b_c)).astype(BF16)
        dec_ref[...] = jnp.exp(b_end)
        qm = (q * jnp.exp(b_c - b_half)).astype(BF16)
        km = (kk * jnp.exp(b_half - b_c)).astype(BF16)
        same_chunk_earlier = ccum > 0
        rows = ccum.shape[0]
        for r in range(tt // rows):
            sl = slice(r * rows, (r + 1) * rows)
            sc = jnp.where(same_chunk_earlier, _dot_nt(qm[sl], km[sl]), 0.0)
            od_ref[sl, :] = _dot(sc.astype(BF16), vb_ref[sl, :])
        recurrence(min(tt, HG_CHUNK), HG_CHUNK_UNROLL)

    @pl.when(jnp.logical_not(safe))
    def _():
        ng = tt // HG_GROUP
        bcum = _blockwise_prefix(log_f, cum_ref[...])
        bend = _blockwise_prefix(log_f, tot_ref[...])
        qd_ref[...] = (q * jnp.exp(bcum)).astype(BF16)
        kd_ref[...] = (kk * jnp.exp(bend - bcum)).astype(BF16)
        dec_ref[...] = jnp.exp(bend)
        b3 = bcum.reshape(ng, HG_GROUP, HEAD_DIM)
        q3 = q.reshape(ng, HG_GROUP, HEAD_DIM)
        k3 = kk.reshape(ng, HG_GROUP, HEAD_DIM)
        v3 = v.reshape(ng, HG_GROUP, HEAD_DIM)
        pos = lax.broadcasted_iota(jnp.int32, (ng, HG_GROUP, HEAD_DIM), 1)
        ones = ones_ref[...]
        od = jnp.zeros((tt, HEAD_DIM), F32)
        for s in range(HG_GROUP):
            diff = b3 - b3[:, s:s + 1, :]
            wgt = jnp.where(pos >= s, jnp.exp(jnp.minimum(diff, 0.0)), 0.0)
            prod = (q3 * k3[:, s:s + 1, :] * wgt).reshape(tt, HEAD_DIM)
            score = _dot(prod.astype(BF16), ones)
            vs = jnp.broadcast_to(v3[:, s:s + 1, :], (ng, HG_GROUP, HEAD_DIM)).reshape(tt, HEAD_DIM)
            od = od + score * vs
        od_ref[...] = od
        recurrence(HG_GROUP, HG_UNROLL)

    o = od_ref[...] + oi_ref[...]
    o = o * lax.rsqrt(jnp.mean(o * o, axis=-1, keepdims=True) + RMS_EPS) * gain_ref[0]
    gg = g_ref[0, 0]
    o_ref[0] = (o * (gg * jax.nn.sigmoid(gg))).astype(o_ref.dtype)

    @pl.when(ti == pl.num_programs(2) - 1)
    def _():
        s_ref[0, 0] = st_ref[...].T


def _span_matrices(rows, span):
    r = jnp.arange(rows)
    same = (r[:, None] // span) == (r[None, :] // span)
    cum = jnp.where(same & (r[None, :] <= r[:, None]), 1.0, 0.0).astype(BF16)
    tot = jnp.where(same, 1.0, 0.0).astype(BF16)
    half = jnp.where(same & ((r[None, :] % span) < span // 2), 1.0, 0.0).astype(BF16)
    return cum, tot, half


def hgrn2(q, a, i, g, lb, s0, gain):
    b, h, t, d = q.shape
    tt = min(t, 1024)
    rows = min(tt, LANES)
    cum, tot, _ = _span_matrices(rows, HG_GROUP)
    ccum, ctot, chalf = _span_matrices(rows, min(tt, HG_CHUNK))
    ones = jnp.ones((d, d), BF16)
    seq_spec = pl.BlockSpec((1, 1, tt, d), lambda bi, hi, ti: (bi, hi, ti, 0))
    head_spec = pl.BlockSpec((1, 1, d), lambda bi, hi, ti: (hi, 0, 0))
    state_spec = pl.BlockSpec((1, 1, d, d), lambda bi, hi, ti: (bi, hi, 0, 0))
    const = lambda shape: pl.BlockSpec(shape, lambda bi, hi, ti: (0, 0))
    return pl.pallas_call(
        _hgrn_kernel,
        grid=(b, h, t // tt),
        in_specs=[seq_spec, seq_spec, seq_spec, seq_spec, head_spec, state_spec, head_spec,
                  const((rows, rows)), const((rows, rows)), const((d, d)),
                  const((rows, rows)), const((rows, rows)), const((rows, rows))],
        out_specs=[pl.BlockSpec((1, tt, d), lambda bi, hi, ti: (bi, ti, hi)), state_spec],
        out_shape=[jax.ShapeDtypeStruct((b, t, h * d), BF16),
                   jax.ShapeDtypeStruct((b, h, d, d), F32)],
        scratch_shapes=([pltpu.VMEM((d, d), F32)] + [pltpu.VMEM((tt, d), BF16)] * 3
                        + [pltpu.VMEM((tt, d), F32)] * 3),
        compiler_params=_params("parallel", "parallel", "arbitrary"),
        name="hgrn2",
    )(q, a, i, g, lb, s0, gain, cum, tot, ones, ccum, ctot, chalf)


def _layer_norm(x, g, b):
    mu = jnp.mean(x, axis=-1, keepdims=True)
    xc = x - mu
    var = jnp.mean(xc * xc, axis=-1, keepdims=True)
    return xc * lax.rsqrt(var + LN_EPS) * g + b


def _out_kernel(oa_ref, ob_ref, wa_ref, wb_ref, x_ref, gate_ref, sc_ref, sh_ref, lng_ref, lnb_ref,
                rw_ref, rb_ref, x1_ref, h2_ref, lg_ref, acc_ref, *, alpha):
    bb, tt, dh = oa_ref.shape
    tn = wa_ref.shape[1]
    n = pl.program_id(2)
    part = (_dot(oa_ref[...].reshape(bb * tt, dh), wa_ref[...])
            + _dot(ob_ref[...].reshape(bb * tt, dh), wb_ref[...]))
    acc_ref[n] = part

    @pl.when(n == pl.num_programs(2) - 1)
    def _():
        d = x_ref.shape[2]
        mix = jnp.concatenate([acc_ref[k] for k in range(d // tn)], axis=1).reshape(bb, tt, d)
        x1 = _layer_norm(alpha * x_ref[...] + gate_ref[...] * mix, lng_ref[...], lnb_ref[...])
        x1_ref[...] = x1
        h2 = x1 * (1.0 + sc_ref[...]) + sh_ref[...]
        h2_ref[...] = h2.astype(h2_ref.dtype)
        hh, hm, hl = _split3(h2.reshape(bb * tt, d))
        w = rw_ref[...]
        wh = w.astype(BF16)
        wl = (w - wh.astype(F32)).astype(BF16)
        lg = (_dot(hh, wh) + (_dot(hm, wh) + _dot(hh, wl))) + (_dot(hl, wh) + _dot(hm, wl))
        lg_ref[...] = (lg + rb_ref[...]).reshape(bb, tt, lg.shape[-1])


def out_projection(oa, ob, wa, wb, x, gate, scale, shift, ln_g, ln_b, router_w, router_b, alpha):
    b, t, dh = oa.shape
    d = x.shape[2]
    e = router_w.shape[1]
    bb, tt = _row_tiles(b, t)
    tt = min(tt, 256)
    tn = min(d, 512)
    row = lambda w_: pl.BlockSpec((bb, tt, w_), lambda i, j, n: (i, j, 0))
    mod = pl.BlockSpec((bb, 1, d), lambda i, j, n: (i, 0, 0))
    vec = lambda w_: pl.BlockSpec((1, w_), lambda i, j, n: (0, 0))
    wspec = pl.BlockSpec((dh, tn), lambda i, j, n: (0, n))
    return pl.pallas_call(
        functools.partial(_out_kernel, alpha=alpha),
        grid=(b // bb, t // tt, d // tn),
        in_specs=[row(dh), row(dh), wspec, wspec, row(d), mod, mod, mod, vec(d), vec(d),
                  pl.BlockSpec((d, e), lambda i, j, n: (0, 0)), vec(e)],
        out_specs=[row(d), row(d), row(e)],
        out_shape=[jax.ShapeDtypeStruct((b, t, d), F32), jax.ShapeDtypeStruct((b, t, d), F32),
                   jax.ShapeDtypeStruct((b, t, e), F32)],
        scratch_shapes=[pltpu.VMEM((d // tn, bb * tt, tn), F32)],
        compiler_params=_params("parallel", "parallel", "arbitrary"),
        name="out_projection",
    )(oa, ob, wa, wb, x, gate, scale, shift, ln_g.reshape(1, d), ln_b.reshape(1, d),
      router_w, router_b.reshape(1, e))


DMA_UNROLL = 8


def _row_copy_loops(copy, n):
    def issue(blk, slot):
        def body(r, c):
            copy(blk, r, slot).start()
            return c
        lax.fori_loop(0, n, body, 0, unroll=DMA_UNROLL)

    def drain(slot):
        def body(r, c):
            copy(0, 0, slot).wait()
            return c
        lax.fori_loop(0, n, body, 0, unroll=DMA_UNROLL)

    return issue, drain


def _gather_kernel(tok_ref, live_ref, src_ref, o_ref, buf_ref, sem):
    i = pl.program_id(0)
    slot = i % 2

    def copy(blk, r, sl):
        return pltpu.make_async_copy(src_ref.at[pl.ds(tok_ref[blk * MOE_SUB + r], 1), :],
                                     buf_ref.at[sl, pl.ds(r, 1), :], sem.at[sl])

    issue, drain = _row_copy_loops(copy, MOE_SUB)

    @pl.when(jnp.logical_and(i == 0, live_ref[0] > 0))
    def _():
        issue(0, 0)

    nxt = jnp.minimum(i + 1, pl.num_programs(0) - 1)

    @pl.when(jnp.logical_and(i + 1 < pl.num_programs(0), live_ref[nxt] > 0))
    def _():
        issue(i + 1, 1 - slot)

    @pl.when(live_ref[i] > 0)
    def _():
        drain(slot)
        o_ref[...] = buf_ref[slot].astype(o_ref.dtype)

    @pl.when(live_ref[i] == 0)
    def _():
        o_ref[...] = jnp.zeros_like(o_ref)


def gather_rows(src, tok, sub_live):
    r = tok.shape[0]
    d = src.shape[1]
    return pl.pallas_call(
        _gather_kernel,
        grid_spec=pltpu.PrefetchScalarGridSpec(
            num_scalar_prefetch=2, grid=(r // MOE_SUB,),
            in_specs=[pl.BlockSpec(memory_space=pl.ANY)],
            out_specs=pl.BlockSpec((MOE_SUB, d), lambda i, tok_, lv_: (i, 0)),
            scratch_shapes=[pltpu.VMEM((2, MOE_SUB, d), F32), pltpu.SemaphoreType.DMA((2,))]),
        out_shape=jax.ShapeDtypeStruct((r, d), BF16),
        compiler_params=pltpu.CompilerParams(dimension_semantics=("arbitrary",),
                                             vmem_limit_bytes=VMEM_LIMIT, disable_bounds_checks=True),
        name="gather_rows",
    )(tok, sub_live, src)


def _expert_changed(be_ref, i):
    return jnp.logical_or(i == 0, be_ref[i] != be_ref[jnp.maximum(i - 1, 0)])


def _stream_expert_weights(be_ref, nb_ref, nx_ref, w_hbms, wf_ref, wb_refs, sem, layer):
    j, i = pl.program_id(0), pl.program_id(1)
    tn = wf_ref.shape[2]
    live = i < nb_ref[0]

    def copies(e):
        cols = pl.ds(pl.multiple_of(j * tn, tn), tn)
        return [pltpu.make_async_copy(w.at[layer, e, :, cols], wf_ref.at[m], sem.at[m])
                for m, w in enumerate(w_hbms)]

    @pl.when(jnp.logical_and(live, i == 0))
    def _():
        for c in copies(be_ref[0]):
            c.start()

    @pl.when(jnp.logical_and(live, _expert_changed(be_ref, i)))
    def _():
        for c in copies(be_ref[i]):
            c.wait()
        for m, wb_ref in enumerate(wb_refs):
            wb_ref[...] = wf_ref[m].astype(BF16)
        nxt = nx_ref[i]

        @pl.when(nxt >= 0)
        def _():
            for c in copies(nxt):
                c.start()

    return live


def _for_filled_rows(live, n_valid, o_ref, compute):
    tm = o_ref.shape[0]
    for rows in range(MOE_SUB, tm + 1, MOE_SUB):
        @pl.when(jnp.logical_and(live, jnp.logical_and(n_valid > rows - MOE_SUB, n_valid <= rows)))
        def _(rows=rows):
            o_ref[:rows, :] = compute(rows).astype(o_ref.dtype)
            if rows < tm:
                o_ref[rows:, :] = jnp.zeros((tm - rows, o_ref.shape[1]), o_ref.dtype)

    @pl.when(jnp.logical_not(live))
    def _():
        o_ref[...] = jnp.zeros_like(o_ref)


def _gate_up_kernel(be_ref, nb_ref, nx_ref, nv_ref, x_ref, wg_hbm, bg_ref, wu_hbm, bu_ref, o_ref,
                    wf_ref, wgb_ref, wub_ref, sem, *, layer):
    live = _stream_expert_weights(be_ref, nb_ref, nx_ref, (wg_hbm, wu_hbm), wf_ref,
                                  (wgb_ref, wub_ref), sem, layer)

    def compute(rows):
        x = x_ref[:rows, :]
        gate = jnp.minimum(_dot(x, wgb_ref[...]) + bg_ref[0], SWIGLU_LIMIT)
        up = jnp.clip(_dot(x, wub_ref[...]) + bu_ref[0], -SWIGLU_LIMIT, SWIGLU_LIMIT)
        return (up + 1.0) * gate * jax.nn.sigmoid(SWIGLU_ALPHA * gate)

    _for_filled_rows(live, nv_ref[pl.program_id(1)], o_ref, compute)


def _down_kernel(be_ref, nb_ref, nx_ref, nv_ref, a_ref, w_hbm, b_ref, o_ref, wf_ref, wb_ref, sem,
                 *, layer):
    live = _stream_expert_weights(be_ref, nb_ref, nx_ref, (w_hbm,), wf_ref, (wb_ref,), sem, layer)

    def compute(rows):
        return _dot(a_ref[:rows, :], wb_ref[...]) + b_ref[0]

    _for_filled_rows(live, nv_ref[pl.program_id(1)], o_ref, compute)


def expert_ffn(xs, block_expert, n_used, next_expert, n_valid, layer,
               w_gate, b_gate, w_up, b_up, w_down, b_down):
    r, d = xs.shape
    f = w_gate.shape[3]
    nb = r // MOE_TM
    hbm = pl.BlockSpec(memory_space=pl.ANY)
    bspec = lambda n_: pl.BlockSpec((1, 1, n_), lambda j, i, be, nu, nx, nv: (be[i], 0, j))
    rows = lambda w_: pl.BlockSpec((MOE_TM, w_), lambda j, i, be, nu, nx, nv: (i, 0))
    tile = lambda n_: pl.BlockSpec((MOE_TM, n_), lambda j, i, be, nu, nx, nv: (i, j))
    tn = min(f, 512)
    act = pl.pallas_call(
        functools.partial(_gate_up_kernel, layer=layer),
        grid_spec=pltpu.PrefetchScalarGridSpec(
            num_scalar_prefetch=4, grid=(f // tn, nb),
            in_specs=[rows(d), hbm, bspec(tn), hbm, bspec(tn)],
            out_specs=tile(tn),
            scratch_shapes=[pltpu.VMEM((2, d, tn), F32), pltpu.VMEM((d, tn), BF16),
                            pltpu.VMEM((d, tn), BF16), pltpu.SemaphoreType.DMA((2,))]),
        out_shape=jax.ShapeDtypeStruct((r, f), BF16),
        compiler_params=_params("arbitrary", "arbitrary"),
        name="expert_gate_up",
    )(block_expert, n_used, next_expert, n_valid, xs, w_gate, b_gate, w_up, b_up)
    tn = min(d, 1024)
    return pl.pallas_call(
        functools.partial(_down_kernel, layer=layer),
        grid_spec=pltpu.PrefetchScalarGridSpec(
            num_scalar_prefetch=4, grid=(d // tn, nb),
            in_specs=[rows(f), hbm, bspec(tn)],
            out_specs=tile(tn),
            scratch_shapes=[pltpu.VMEM((1, f, tn), F32), pltpu.VMEM((f, tn), BF16),
                            pltpu.SemaphoreType.DMA((1,))]),
        out_shape=jax.ShapeDtypeStruct((r, d), F32),
        compiler_params=_params("arbitrary", "arbitrary"),
        name="expert_down",
    )(block_expert, n_used, next_expert, n_valid, act, w_down, b_down)


COMBINE_TOKENS = 64


def _combine_kernel(pos_ref, ys_ref, x1_ref, rg_ref, gate_ref, lng_ref, lnb_ref, o_ref, buf_ref, sem,
                    *, alpha, tok_offset):
    _, tt, d = x1_ref.shape
    step = pl.program_id(0) * pl.num_programs(1) + pl.program_id(1)
    n_steps = pl.num_programs(0) * pl.num_programs(1)
    slot = step % 2

    def copy(stp, r, sl):
        t, k = r // TOP_K, r % TOP_K
        src_row = pos_ref[(tok_offset + stp * tt) * TOP_K + r]
        return pltpu.make_async_copy(ys_ref.at[pl.ds(src_row, 1), :],
                                     buf_ref.at[sl, k, pl.ds(t, 1), :], sem.at[sl])

    issue, drain = _row_copy_loops(copy, tt * TOP_K)

    @pl.when(step == 0)
    def _():
        issue(0, 0)

    @pl.when(step + 1 < n_steps)
    def _():
        issue(step + 1, 1 - slot)

    drain(slot)
    rg = rg_ref[0]
    ffn = ((rg[:, 0:1] * buf_ref[slot, 0] + rg[:, 1:2] * buf_ref[slot, 1])
           + (rg[:, 2:3] * buf_ref[slot, 2] + rg[:, 3:4] * buf_ref[slot, 3]))
    y = alpha * x1_ref[0] + gate_ref[0] * ffn
    o_ref[0] = _layer_norm(y, lng_ref[...], lnb_ref[...])


def combine(ys, pos, router_gates, x1, gate, ln_g, ln_b, alpha, tok_offset):
    b, t, d = x1.shape
    tt = min(t, COMBINE_TOKENS)
    row = lambda w_: pl.BlockSpec((1, tt, w_), lambda i, j, p: (i, j, 0))
    mod = pl.BlockSpec((1, 1, d), lambda i, j, p: (i, 0, 0))
    vec = pl.BlockSpec((1, d), lambda i, j, p: (0, 0))
    return pl.pallas_call(
        functools.partial(_combine_kernel, alpha=alpha, tok_offset=tok_offset),
        grid_spec=pltpu.PrefetchScalarGridSpec(
            num_scalar_prefetch=1, grid=(b, t // tt),
            in_specs=[pl.BlockSpec(memory_space=pl.ANY), row(d), row(TOP_K), mod, vec, vec],
            out_specs=row(d),
            scratch_shapes=[pltpu.VMEM((2, TOP_K, tt, d), F32), pltpu.SemaphoreType.DMA((2,))]),
        out_shape=jax.ShapeDtypeStruct((b, t, d), F32),
        compiler_params=pltpu.CompilerParams(dimension_semantics=("arbitrary", "arbitrary"),
                                             vmem_limit_bytes=VMEM_LIMIT, disable_bounds_checks=True),
        name="moe_combine",
    )(pos, ys, x1, router_gates, gate, ln_g.reshape(1, d), ln_b.reshape(1, d))


def _route(logits):
    n_tok, n_exp = logits.shape
    top_logit, top_e = lax.top_k(logits, TOP_K)
    gates = jax.nn.softmax(top_logit, axis=-1)
    nk = n_tok * TOP_K
    flat_e = top_e.reshape(nk).astype(jnp.int32)
    onehot = (flat_e[:, None] == jnp.arange(n_exp, dtype=jnp.int32)[None, :]).astype(jnp.int32)
    running = jnp.cumsum(onehot, axis=0)
    counts = running[-1]
    rank = jnp.sum(running * onehot, axis=1) - 1
    padded = (counts + MOE_TM - 1) // MOE_TM * MOE_TM
    padded_end = jnp.cumsum(padded)
    padded_start = padded_end - padded
    start = jnp.cumsum(counts) - counts
    pos = (padded_start[flat_e] + rank).astype(jnp.int32)
    n_blocks = -(-nk // MOE_TM) + n_exp
    block_start = jnp.arange(n_blocks, dtype=jnp.int32) * MOE_TM
    block_expert = jnp.minimum(jnp.sum(block_start[:, None] >= padded_end[None, :], axis=1),
                               n_exp - 1).astype(jnp.int32)
    n_used = (padded_end[-1:] // MOE_TM).astype(jnp.int32)
    blk = jnp.arange(n_blocks, dtype=jnp.int32)
    run_start = (blk < n_used[0]) & ((blk == 0) | (block_expert != jnp.roll(block_expert, 1)))
    later_start = lax.cummin(jnp.where(run_start, blk, n_blocks)[::-1])[::-1]
    next_start = jnp.concatenate([later_start[1:], jnp.full((1,), n_blocks, jnp.int32)])
    next_expert = jnp.where(next_start < n_blocks,
                            block_expert[jnp.minimum(next_start, n_blocks - 1)], -1).astype(jnp.int32)
    order = jnp.argsort(flat_e).astype(jnp.int32)
    slot = jnp.arange(n_blocks * MOE_TM, dtype=jnp.int32)
    slot_e = jnp.repeat(block_expert, MOE_TM)
    within = slot - padded_start[slot_e]
    valid = within < counts[slot_e]
    src = order[jnp.clip(start[slot_e] + within, 0, nk - 1)]
    slot_tok = jnp.where(valid, src // TOP_K, 0).astype(jnp.int32)
    n_valid = jnp.clip(counts[block_expert] - (block_start - padded_start[block_expert]), 0, MOE_TM)
    n_valid = jnp.where(blk < n_used[0], n_valid, 0).astype(jnp.int32)
    sub_live = valid.reshape(-1, MOE_SUB)[:, 0].astype(jnp.int32)
    return slot_tok, gates, block_expert, n_used, next_expert, n_valid, sub_live, pos


def kernel(x_prompt, x_sample, c_prompt, c_sample, cache_sb_k, cache_sb_v, state_hgrn,
           w_ada, b_ada, w_in, w_out, norm_a, norm_b, lb_logits,
           ln1_g, ln1_b, ln2_g, ln2_b, router_w, router_b,
           w_gate, b_gate, w_up, b_up, w_down, b_down):
    depth = w_ada.shape[0]
    d = x_prompt.shape[-1]
    alpha = (2.0 * depth) ** 0.25
    d_sb = cache_sb_k.shape[2] * HEAD_DIM
    hg_heads = state_hgrn.shape[2]
    d_hk = hg_heads * state_hgrn.shape[3]
    d_hg = hg_heads * state_hgrn.shape[4]
    sb_scale = HEAD_DIM ** -0.5
    n_exp = router_w.shape[-1]
    lower_bounds = jnp.cumsum(jax.nn.softmax(lb_logits.astype(F32), axis=0), axis=0)
    streams = [x_prompt, x_sample]
    conds = [c_prompt, c_sample]
    n_rows = [c.shape[0] for c in conds]
    pad = (-sum(n_rows)) % 8
    c_all = jnp.concatenate(conds + [jnp.zeros((pad, d), F32)], axis=0)
    outs = [[] for _ in range(6)]

    for l in range(depth):
        mods = ada_modulation(c_all, w_ada[l], b_ada[l])
        w_in_l = w_in[l]
        seg = [0, d_sb, 2 * d_sb, 3 * d_sb, 3 * d_sb + d_hk, 3 * d_sb + 2 * d_hk,
               3 * d_sb + 2 * d_hk + d_hg, 3 * d_sb + 2 * d_hk + 2 * d_hg]
        w_seg = [w_in_l[:, seg[i]:seg[i + 1]].astype(BF16) for i in range(7)]
        wo_a = w_out[l, :d_sb].astype(BF16)
        wo_b = w_out[l, d_sb:].astype(BF16)
        gain_a = norm_a[l].reshape(-1, 1, HEAD_DIM)
        gain_b = norm_b[l].reshape(-1, 1, HEAD_DIM)
        lb = lower_bounds[l].reshape(hg_heads, 1, HEAD_DIM)
        x1s, h2s, lgs, gate2s = [], [], [], []
        row0 = 0
        for si, x in enumerate(streams):
            b = x.shape[0]
            m = mods[row0:row0 + b].reshape(b, N_MOD, 1, d)
            row0 += b
            shift1, scale1, gate1, shift2, scale2, gate2 = [m[:, i] for i in range(N_MOD)]
            h = modulate(x, scale1, shift1)
            (q_a,) = project_heads(h, w_seg[0], [BF16], scale=sb_scale)
            k_f, k_h = project_heads(h, w_seg[1], [F32, BF16])
            v_f, v_h = project_heads(h, w_seg[2], [F32, BF16])
            (q_b,) = project_heads(h, w_seg[3], [F32])
            (f_b,) = project_heads(h, w_seg[4], [F32])
            (i_b,) = project_heads(h, w_seg[5], [F32])
            (g_b,) = project_heads(h, w_seg[6], [F32])
            if si == 0:
                o_a = stick_breaking(q_a, k_h, v_h, gain_a)
                s0 = jnp.zeros((b, hg_heads, d_hk // hg_heads, HEAD_DIM), F32)
            else:
                o_a = stick_breaking(q_a, k_h, v_h, gain_a, cache_sb_k[l], cache_sb_v[l])
                s0 = state_hgrn[l]
            o_b, s_new = hgrn2(q_b, f_b, i_b, g_b, lb, s0, gain_b)
            x1, h2, lg = out_projection(o_a, o_b, wo_a, wo_b, x, gate1, scale2, shift2,
                                        ln1_g[l], ln1_b[l], router_w[l], router_b[l], alpha)
            x1s.append(x1)
            h2s.append(h2.reshape(-1, d))
            lgs.append(lg.reshape(-1, n_exp))
            gate2s.append(gate2)
            outs[3 * si + 0].append(k_f)
            outs[3 * si + 1].append(v_f)
            outs[3 * si + 2].append(s_new)

        (slot_tok, router_gates, block_expert, n_used, next_expert, n_valid, sub_live,
         pos) = _route(jnp.concatenate(lgs, axis=0))
        xs = gather_rows(jnp.concatenate(h2s, axis=0), slot_tok, sub_live)
        ys = expert_ffn(xs, block_expert, n_used, next_expert, n_valid, l,
                        w_gate, b_gate[l][:, None, :], w_up, b_up[l][:, None, :],
                        w_down, b_down[l][:, None, :])
        tok0 = 0
        new_streams = []
        for si, x1 in enumerate(x1s):
            n_s = x1.shape[0] * x1.shape[1]
            rg = router_gates[tok0:tok0 + n_s].reshape(x1.shape[0], x1.shape[1], TOP_K)
            new_streams.append(combine(ys, pos, rg, x1, gate2s[si], ln2_g[l], ln2_b[l], alpha, tok0))
            tok0 += n_s
        streams = new_streams

    stack = lambda xs_: jnp.stack(xs_)
    return (streams[0], streams[1], stack(outs[0]), stack(outs[1]), stack(outs[2]),
            stack(outs[3]), stack(outs[4]), stack(outs[5]))
```

```python
import functools

import jax
import jax.numpy as jnp
from jax import lax
from jax.experimental import pallas as pl
from jax.experimental.pallas import tpu as pltpu

HEAD_DIM = 128
TOP_K = 4
SWIGLU_LIMIT = 7.0
SWIGLU_ALPHA = 1.702
LN_EPS = 1e-5
RMS_EPS = 1e-6
N_MOD = 6

LANES = 128
SB_TILE = 256
SB_DEAD = -104.0
HG_GROUP = 16
HG_UNROLL = 8
HG_CHUNK = 64
HG_CHUNK_UNROLL = 4
HG_SAFE = 60.0
MOE_TM = 1024
MOE_SUB = 256
CAST_ROWS = 64
VMEM_LIMIT = 56 * 1024 * 1024

F32 = jnp.float32
BF16 = jnp.bfloat16


def _params(*sem):
    return pltpu.CompilerParams(dimension_semantics=sem, vmem_limit_bytes=VMEM_LIMIT)


def _dot(a, b):
    return jnp.dot(a, b, preferred_element_type=F32)


def _dot_nt(a, b):
    return lax.dot_general(a, b, (((1,), (1,)), ((), ())), preferred_element_type=F32)


def _dot_tn(a, b):
    return lax.dot_general(a, b, (((0,), (0,)), ((), ())), preferred_element_type=F32)


def _split3(x):
    hi = x.astype(BF16)
    r1 = x - hi.astype(F32)
    mid = r1.astype(BF16)
    lo = (r1 - mid.astype(F32)).astype(BF16)
    return hi, mid, lo


def _dot_exact_rhs(x, m):
    hi, mid, lo = _split3(x)
    return _dot(m, hi) + _dot(m, mid) + _dot(m, lo)


def _softplus(z):
    return jnp.maximum(z, 0.0) + jnp.log(1.0 + jnp.exp(-jnp.abs(z)))


def _ada_kernel(c_ref, w_ref, b_ref, o_ref):
    c = c_ref[...]
    s = (c * jax.nn.sigmoid(c)).astype(BF16)
    o_ref[...] = _dot(s, w_ref[...].astype(BF16)) + b_ref[...]


def ada_modulation(c, w, b):
    r, d = c.shape
    n = w.shape[1]
    tn = min(n, 512)
    return pl.pallas_call(
        _ada_kernel,
        grid=(n // tn,),
        in_specs=[pl.BlockSpec((r, d), lambda j: (0, 0)),
                  pl.BlockSpec((d, tn), lambda j: (0, j)),
                  pl.BlockSpec((1, tn), lambda j: (0, j))],
        out_specs=pl.BlockSpec((r, tn), lambda j: (0, j)),
        out_shape=jax.ShapeDtypeStruct((r, n), F32),
        compiler_params=_params("arbitrary"),
        name="ada_modulation",
    )(c, w, b.reshape(1, n))


def _modulate_kernel(x_ref, sc_ref, sh_ref, o_ref):
    o_ref[...] = (x_ref[...] * (1.0 + sc_ref[...]) + sh_ref[...]).astype(o_ref.dtype)


def _row_tiles(b, t):
    tt = min(t, 512)
    bb = max(1, min(b, 256 // tt)) if tt < 256 else 1
    return bb, tt


def modulate(x, scale, shift):
    b, t, d = x.shape
    bb, tt = _row_tiles(b, t)
    mod_spec = pl.BlockSpec((bb, 1, d), lambda i, j: (i, 0, 0))
    return pl.pallas_call(
        _modulate_kernel,
        grid=(b // bb, t // tt),
        in_specs=[pl.BlockSpec((bb, tt, d), lambda i, j: (i, j, 0)), mod_spec, mod_spec],
        out_specs=pl.BlockSpec((bb, tt, d), lambda i, j: (i, j, 0)),
        out_shape=jax.ShapeDtypeStruct((b, t, d), BF16),
        compiler_params=_params("parallel", "parallel"),
        name="modulate",
    )(x, scale, shift)


def _proj_kernel(h_ref, w_ref, *o_refs, scale):
    bb, tt, d = h_ref.shape
    acc = _dot(h_ref[...].reshape(bb * tt, d), w_ref[...])
    if scale != 1.0:
        acc = acc * scale
    heads = o_refs[0].shape[1]
    for hh in range(heads):
        blk = acc[:, hh * HEAD_DIM:(hh + 1) * HEAD_DIM].reshape(bb, tt, HEAD_DIM)
        for o_ref in o_refs:
            o_ref[:, hh] = blk.astype(o_ref.dtype)


def project_heads(h, w, out_dtypes, scale=1.0):
    b, t, d = h.shape
    n = w.shape[1]
    heads = n // HEAD_DIM
    bb, tt = _row_tiles(b, t)
    tn = min(n, 1024)
    hpb = tn // HEAD_DIM
    out_spec = pl.BlockSpec((bb, hpb, tt, HEAD_DIM), lambda i, j, k: (i, k, j, 0))
    outs = pl.pallas_call(
        functools.partial(_proj_kernel, scale=scale),
        grid=(b // bb, t // tt, n // tn),
        in_specs=[pl.BlockSpec((bb, tt, d), lambda i, j, k: (i, j, 0)),
                  pl.BlockSpec((d, tn), lambda i, j, k: (0, k))],
        out_specs=[out_spec] * len(out_dtypes),
        out_shape=[jax.ShapeDtypeStruct((b, heads, t, HEAD_DIM), dt) for dt in out_dtypes],
        compiler_params=_params("parallel", "parallel", "arbitrary"),
        name="project_heads",
    )(h, w)
    return outs


def _sb_block(q, k, v, u_ext, carry, row0):
    tk = k.shape[0]
    masked = row0 is not None
    z = _dot_nt(q, k)
    lf = -_softplus(z)
    if masked:
        row = lax.broadcasted_iota(jnp.int32, z.shape, 0) + row0
        col = lax.broadcasted_iota(jnp.int32, z.shape, 1)
        earlier = col < row
        lf = jnp.where(earlier, lf, 0.0)
    hi = lf.astype(BF16)
    lo = (lf - hi.astype(F32)).astype(BF16)
    cs = _dot(hi, u_ext) + _dot(lo, u_ext)
    if tk >= LANES:
        later = jnp.concatenate([carry] * (tk // LANES), axis=1)
    else:
        later = carry[:, :tk]
    after = cs[:, LANES:] + later
    w = jnp.exp(z + lf + after)
    if masked:
        w = jnp.where(earlier, w, 0.0)
    out = _dot(w.astype(BF16), v)
    return out, carry + cs[:, :LANES]


def _sb_kernel(q_ref, kd_ref, vd_ref, kp_ref, vp_ref, ud_ref, up_ref, gain_ref, o_ref,
               acc_ref, carry_ref, *, past_blocks):
    tq = q_ref.shape[2]
    tk = up_ref.shape[0]
    q = q_ref[0, 0]
    n_past = pl.program_id(2) * (tq // tk) if past_blocks is None else past_blocks

    def own_keys():
        return _sb_block(q, kd_ref[0, 0].astype(BF16), vd_ref[0, 0].astype(BF16), ud_ref[...],
                         jnp.zeros((tq, LANES), F32), row0=0)

    def past_block(kb, carry):
        start = pl.multiple_of(kb * tk, tk)
        k = kp_ref[0, 0, pl.ds(start, tk), :].astype(BF16)
        v = vp_ref[0, 0, pl.ds(start, tk), :].astype(BF16)
        return _sb_block(q, k, v, up_ref[...], carry, row0=None)

    def own_and_nearest():
        out_d, carry_d = own_keys()
        out_p, carry_p = past_block(n_past - 1, carry_d)
        acc_ref[...] = out_d + out_p
        carry_ref[...] = carry_p

    def own_only():
        out_d, carry_d = own_keys()
        acc_ref[...] = out_d
        carry_ref[...] = carry_d

    if past_blocks is None:
        pl.when(n_past > 0)(own_and_nearest)
        pl.when(n_past == 0)(own_only)
    elif past_blocks > 0:
        own_and_nearest()
    else:
        own_only()

    def cond(state):
        kb, alive = state
        return jnp.logical_and(kb >= 0, alive)

    def body(state):
        kb, _ = state
        out, carry = past_block(kb, carry_ref[...])
        acc_ref[...] += out
        carry_ref[...] = carry
        return kb - 1, jnp.max(carry) > SB_DEAD

    lax.while_loop(cond, body, (n_past - 2, jnp.max(carry_ref[...]) > SB_DEAD))
    o = acc_ref[...]
    o = o * lax.rsqrt(jnp.mean(o * o, axis=-1, keepdims=True) + RMS_EPS)
    o_ref[0] = (o * gain_ref[0]).astype(o_ref.dtype)


def _suffix_matrix(tk):
    j = jnp.arange(tk)[:, None]
    c = jnp.arange(LANES + tk)[None, :]
    return jnp.where((c < LANES) | (j > c - LANES), 1.0, 0.0).astype(BF16)


def stick_breaking(q, k, v, gain, past_k=None, past_v=None):
    b, h, t, d = q.shape
    if past_k is None:
        tq = min(t, SB_TILE)
        tk = tq
        kp, vp, past_blocks = k, v, None
    else:
        tq = t
        tk = min(past_k.shape[2], SB_TILE)
        kp, vp, past_blocks = past_k, past_v, past_k.shape[2] // tk
    p = kp.shape[2]
    tile_spec = pl.BlockSpec((1, 1, tq, d), lambda i, j, m: (i, j, m, 0))
    past_spec = pl.BlockSpec((1, 1, p, d), lambda i, j, m: (i, j, 0, 0))
    return pl.pallas_call(
        functools.partial(_sb_kernel, past_blocks=past_blocks),
        grid=(b, h, t // tq),
        in_specs=[tile_spec, tile_spec, tile_spec, past_spec, past_spec,
                  pl.BlockSpec((tq, tq + LANES), lambda i, j, m: (0, 0)),
                  pl.BlockSpec((tk, tk + LANES), lambda i, j, m: (0, 0)),
                  pl.BlockSpec((1, 1, d), lambda i, j, m: (j, 0, 0))],
        out_specs=pl.BlockSpec((1, tq, d), lambda i, j, m: (i, m, j)),
        out_shape=jax.ShapeDtypeStruct((b, t, h * d), BF16),
        scratch_shapes=[pltpu.VMEM((tq, LANES), F32), pltpu.VMEM((tq, LANES), F32)],
        compiler_params=_params("parallel", "parallel", "arbitrary"),
        name="stick_breaking",
    )(q, k, v, kp, vp, _suffix_matrix(tq), _suffix_matrix(tk), gain)


def _blockwise_prefix(x, m):
    rows = m.shape[0]
    parts = [_dot_exact_rhs(x[r * rows:(r + 1) * rows], m) for r in range(x.shape[0] // rows)]
    return jnp.concatenate(parts, axis=0) if len(parts) > 1 else parts[0]


def _hgrn_kernel(q_ref, a_ref, i_ref, g_ref, lb_ref, s0_ref, gain_ref, cum_ref, tot_ref, ones_ref,
                 ccum_ref, ctot_ref, chalf_ref,
                 o_ref, s_ref, st_ref, qd_ref, kd_ref, vb_ref, dec_ref, od_ref, oi_ref):
    tt = q_ref.shape[2]
    ti = pl.program_id(2)

    @pl.when(ti == 0)
    def _():
        st_ref[...] = s0_ref[0, 0].T

    lb = lb_ref[0]
    a = a_ref[0, 0]
    log_sig = -_softplus(-a)
    x = jnp.log(lb)
    y = jnp.log1p(-lb) + log_sig
    log_f = jnp.maximum(x, y) + jnp.log(1.0 + jnp.exp(-jnp.abs(x - y)))
    kk = (1.0 - lb) * jax.nn.sigmoid(-a)
    q = q_ref[0, 0]
    v = i_ref[0, 0]

    vb_ref[...] = v.astype(BF16)

    b_half = _blockwise_prefix(log_f, chalf_ref[...])
    b_end = _blockwise_prefix(log_f, ctot_ref[...])
    safe = jnp.logical_and(jnp.min(b_half) > -HG_SAFE, jnp.min(b_end - b_half) > -HG_SAFE)

    def recurrence(step_rows, unroll):
        def step(i, st):
            r0 = pl.multiple_of(i * step_rows, step_rows)
            rows_i = pl.ds(r0, step_rows)
            oi_ref[rows_i, :] = _dot_nt(qd_ref[rows_i, :], st.astype(BF16))
            upd = _dot_tn(vb_ref[rows_i, :], kd_ref[rows_i, :])
            return st * dec_ref[pl.ds(r0, 1), :] + upd
        n = tt // step_rows
        st_ref[...] = lax.fori_loop(0, n, step, st_ref[...], unroll=min(n, unroll))

    @pl.when(safe)
    def _():
        ccum = ccum_ref[...]
        b_c = _blockwise_prefix(log_f, ccum)
        qd_ref[...] = (q * jnp.exp(b_c)).astype(BF16)
        kd_ref[...] = (kk * jnp.exp(b_end - b_c)).astype(BF16)
        dec_ref[...] = jnp.exp(b_end)
        qm = (q * jnp.exp(b_c - b_half)).astype(BF16)
        km = (kk * jnp.exp(b_half - b_c)).astype(BF16)
        same_chunk_earlier = ccum > 0
        rows = ccum.shape[0]
        for r in range(tt // rows):
            sl = slice(r * rows, (r + 1) * rows)
            sc = jnp.where(same_chunk_earlier, _dot_nt(qm[sl], km[sl]), 0.0)
            od_ref[sl, :] = _dot(sc.astype(BF16), vb_ref[sl, :])
        recurrence(min(tt, HG_CHUNK), HG_CHUNK_UNROLL)

    @pl.when(jnp.logical_not(safe))
    def _():
        ng = tt // HG_GROUP
        bcum = _blockwise_prefix(log_f, cum_ref[...])
        bend = _blockwise_prefix(log_f, tot_ref[...])
        qd_ref[...] = (q * jnp.exp(bcum)).astype(BF16)
        kd_ref[...] = (kk * jnp.exp(bend - bcum)).astype(BF16)
        dec_ref[...] = jnp.exp(bend)
        b3 = bcum.reshape(ng, HG_GROUP, HEAD_DIM)
        q3 = q.reshape(ng, HG_GROUP, HEAD_DIM)
        k3 = kk.reshape(ng, HG_GROUP, HEAD_DIM)
        v3 = v.reshape(ng, HG_GROUP, HEAD_DIM)
        pos = lax.broadcasted_iota(jnp.int32, (ng, HG_GROUP, HEAD_DIM), 1)
        ones = ones_ref[...]
        od = jnp.zeros((tt, HEAD_DIM), F32)
        for s in range(HG_GROUP):
            diff = b3 - b3[:, s:s + 1, :]
            wgt = jnp.where(pos >= s, jnp.exp(jnp.minimum(diff, 0.0)), 0.0)
            prod = (q3 * k3[:, s:s + 1, :] * wgt).reshape(tt, HEAD_DIM)
            score = _dot(prod.astype(BF16), ones)
            vs = jnp.broadcast_to(v3[:, s:s + 1, :], (ng, HG_GROUP, HEAD_DIM)).reshape(tt, HEAD_DIM)
            od = od + score * vs
        od_ref[...] = od
        recurrence(HG_GROUP, HG_UNROLL)

    o = od_ref[...] + oi_ref[...]
    o = o * lax.rsqrt(jnp.mean(o * o, axis=-1, keepdims=True) + RMS_EPS) * gain_ref[0]
    gg = g_ref[0, 0]
    o_ref[0] = (o * (gg * jax.nn.sigmoid(gg))).astype(o_ref.dtype)

    @pl.when(ti == pl.num_programs(2) - 1)
    def _():
        s_ref[0, 0] = st_ref[...].T


def _span_matrices(rows, span):
    r = jnp.arange(rows)
    same = (r[:, None] // span) == (r[None, :] // span)
    cum = jnp.where(same & (r[None, :] <= r[:, None]), 1.0, 0.0).astype(BF16)
    tot = jnp.where(same, 1.0, 0.0).astype(BF16)
    half = jnp.where(same & ((r[None, :] % span) < span // 2), 1.0, 0.0).astype(BF16)
    return cum, tot, half


def hgrn2(q, a, i, g, lb, s0, gain):
    b, h, t, d = q.shape
    tt = min(t, 1024)
    rows = min(tt, LANES)
    cum, tot, _ = _span_matrices(rows, HG_GROUP)
    ccum, ctot, chalf = _span_matrices(rows, min(tt, HG_CHUNK))
    ones = jnp.ones((d, d), BF16)
    seq_spec = pl.BlockSpec((1, 1, tt, d), lambda bi, hi, ti: (bi, hi, ti, 0))
    head_spec = pl.BlockSpec((1, 1, d), lambda bi, hi, ti: (hi, 0, 0))
    state_spec = pl.BlockSpec((1, 1, d, d), lambda bi, hi, ti: (bi, hi, 0, 0))
    const = lambda shape: pl.BlockSpec(shape, lambda bi, hi, ti: (0, 0))
    return pl.pallas_call(
        _hgrn_kernel,
        grid=(b, h, t // tt),
        in_specs=[seq_spec, seq_spec, seq_spec, seq_spec, head_spec, state_spec, head_spec,
                  const((rows, rows)), const((rows, rows)), const((d, d)),
                  const((rows, rows)), const((rows, rows)), const((rows, rows))],
        out_specs=[pl.BlockSpec((1, tt, d), lambda bi, hi, ti: (bi, ti, hi)), state_spec],
        out_shape=[jax.ShapeDtypeStruct((b, t, h * d), BF16),
                   jax.ShapeDtypeStruct((b, h, d, d), F32)],
        scratch_shapes=([pltpu.VMEM((d, d), F32)] + [pltpu.VMEM((tt, d), BF16)] * 3
                        + [pltpu.VMEM((tt, d), F32)] * 3),
        compiler_params=_params("parallel", "parallel", "arbitrary"),
        name="hgrn2",
    )(q, a, i, g, lb, s0, gain, cum, tot, ones, ccum, ctot, chalf)


def _layer_norm(x, g, b):
    mu = jnp.mean(x, axis=-1, keepdims=True)
    xc = x - mu
    var = jnp.mean(xc * xc, axis=-1, keepdims=True)
    return xc * lax.rsqrt(var + LN_EPS) * g + b


def _out_kernel(oa_ref, ob_ref, wa_ref, wb_ref, x_ref, gate_ref, sc_ref, sh_ref, lng_ref, lnb_ref,
                rw_ref, rb_ref, x1_ref, h2_ref, lg_ref, acc_ref, *, alpha):
    bb, tt, dh = oa_ref.shape
    tn = wa_ref.shape[1]
    n = pl.program_id(2)
    part = (_dot(oa_ref[...].reshape(bb * tt, dh), wa_ref[...])
            + _dot(ob_ref[...].reshape(bb * tt, dh), wb_ref[...]))
    acc_ref[n] = part

    @pl.when(n == pl.num_programs(2) - 1)
    def _():
        d = x_ref.shape[2]
        mix = jnp.concatenate([acc_ref[k] for k in range(d // tn)], axis=1).reshape(bb, tt, d)
        x1 = _layer_norm(alpha * x_ref[...] + gate_ref[...] * mix, lng_ref[...], lnb_ref[...])
        x1_ref[...] = x1
        h2 = x1 * (1.0 + sc_ref[...]) + sh_ref[...]
        h2_ref[...] = h2.astype(h2_ref.dtype)
        hh, hm, hl = _split3(h2.reshape(bb * tt, d))
        w = rw_ref[...]
        wh = w.astype(BF16)
        wl = (w - wh.astype(F32)).astype(BF16)
        lg = (_dot(hh, wh) + (_dot(hm, wh) + _dot(hh, wl))) + (_dot(hl, wh) + _dot(hm, wl))
        lg_ref[...] = (lg + rb_ref[...]).reshape(bb, tt, lg.shape[-1])


def out_projection(oa, ob, wa, wb, x, gate, scale, shift, ln_g, ln_b, router_w, router_b, alpha):
    b, t, dh = oa.shape
    d = x.shape[2]
    e = router_w.shape[1]
    bb, tt = _row_tiles(b, t)
    tt = min(tt, 256)
    tn = min(d, 512)
    row = lambda w_: pl.BlockSpec((bb, tt, w_), lambda i, j, n: (i, j, 0))
    mod = pl.BlockSpec((bb, 1, d), lambda i, j, n: (i, 0, 0))
    vec = lambda w_: pl.BlockSpec((1, w_), lambda i, j, n: (0, 0))
    wspec = pl.BlockSpec((dh, tn), lambda i, j, n: (0, n))
    return pl.pallas_call(
        functools.partial(_out_kernel, alpha=alpha),
        grid=(b // bb, t // tt, d // tn),
        in_specs=[row(dh), row(dh), wspec, wspec, row(d), mod, mod, mod, vec(d), vec(d),
                  pl.BlockSpec((d, e), lambda i, j, n: (0, 0)), vec(e)],
        out_specs=[row(d), row(d), row(e)],
        out_shape=[jax.ShapeDtypeStruct((b, t, d), F32), jax.ShapeDtypeStruct((b, t, d), F32),
                   jax.ShapeDtypeStruct((b, t, e), F32)],
        scratch_shapes=[pltpu.VMEM((d // tn, bb * tt, tn), F32)],
        compiler_params=_params("parallel", "parallel", "arbitrary"),
        name="out_projection",
    )(oa, ob, wa, wb, x, gate, scale, shift, ln_g.reshape(1, d), ln_b.reshape(1, d),
      router_w, router_b.reshape(1, e))


DMA_UNROLL = 8


def _row_copy_loops(copy, n):
    def issue(blk, slot):
        def body(g, c):
            r0 = pl.multiple_of(g * DMA_UNROLL, DMA_UNROLL)
            for u in range(DMA_UNROLL):
                copy(blk, r0, u, slot).start()
            return c
        lax.fori_loop(0, n // DMA_UNROLL, body, 0)

    def drain(slot):
        def body(g, c):
            for _ in range(DMA_UNROLL):
                copy(0, 0, 0, slot).wait()
            return c
        lax.fori_loop(0, n // DMA_UNROLL, body, 0)

    return issue, drain


def _gather_kernel(tok_ref, live_ref, src_ref, o_ref, buf_ref, sem):
    i = pl.program_id(0)
    slot = i % 2

    def copy(blk, r0, u, sl):
        return pltpu.make_async_copy(src_ref.at[pl.ds(tok_ref[blk * MOE_SUB + r0 + u], 1), :],
                                     buf_ref.at[sl, pl.ds(r0 + u, 1), :], sem.at[sl])

    issue, drain = _row_copy_loops(copy, MOE_SUB)

    @pl.when(jnp.logical_and(i == 0, live_ref[0] > 0))
    def _():
        issue(0, 0)

    nxt = jnp.minimum(i + 1, pl.num_programs(0) - 1)

    @pl.when(jnp.logical_and(i + 1 < pl.num_programs(0), live_ref[nxt] > 0))
    def _():
        issue(i + 1, 1 - slot)

    @pl.when(live_ref[i] > 0)
    def _():
        drain(slot)
        o_ref[...] = buf_ref[slot].astype(o_ref.dtype)

    @pl.when(live_ref[i] == 0)
    def _():
        o_ref[...] = jnp.zeros_like(o_ref)


def gather_rows(src, tok, sub_live):
    r = tok.shape[0]
    d = src.shape[1]
    return pl.pallas_call(
        _gather_kernel,
        grid_spec=pltpu.PrefetchScalarGridSpec(
            num_scalar_prefetch=2, grid=(r // MOE_SUB,),
            in_specs=[pl.BlockSpec(memory_space=pl.ANY)],
            out_specs=pl.BlockSpec((MOE_SUB, d), lambda i, tok_, lv_: (i, 0)),
            scratch_shapes=[pltpu.VMEM((2, MOE_SUB, d), F32), pltpu.SemaphoreType.DMA((2,))]),
        out_shape=jax.ShapeDtypeStruct((r, d), BF16),
        compiler_params=pltpu.CompilerParams(dimension_semantics=("arbitrary",),
                                             vmem_limit_bytes=VMEM_LIMIT, disable_bounds_checks=True),
        name="gather_rows",
    )(tok, sub_live, src)


def _expert_changed(be_ref, i):
    return jnp.logical_or(i == 0, be_ref[i] != be_ref[jnp.maximum(i - 1, 0)])


def _stream_expert_weights(be_ref, nb_ref, nx_ref, w_hbms, wf_ref, wb_refs, sem, layer):
    j, i = pl.program_id(0), pl.program_id(1)
    tn = wf_ref.shape[2]
    live = i < nb_ref[0]

    def copies(e):
        cols = pl.ds(pl.multiple_of(j * tn, tn), tn)
        return [pltpu.make_async_copy(w.at[layer, e, :, cols], wf_ref.at[m], sem.at[m])
                for m, w in enumerate(w_hbms)]

    @pl.when(jnp.logical_and(live, i == 0))
    def _():
        for c in copies(be_ref[0]):
            c.start()

    @pl.when(jnp.logical_and(live, _expert_changed(be_ref, i)))
    def _():
        for c in copies(be_ref[i]):
            c.wait()
        def convert(c, carry):
            rows = pl.ds(pl.multiple_of(c * CAST_ROWS, CAST_ROWS), CAST_ROWS)
            for m, wb_ref in enumerate(wb_refs):
                wb_ref[rows, :] = wf_ref[m, rows, :].astype(BF16)
            return carry

        lax.fori_loop(0, wf_ref.shape[1] // CAST_ROWS, convert, 0)
        nxt = nx_ref[i]

        @pl.when(nxt >= 0)
        def _():
            for c in copies(nxt):
                c.start()

    return live


def _for_filled_rows(live, n_valid, o_ref, compute):
    tm = o_ref.shape[0]
    for rows in range(MOE_SUB, tm + 1, MOE_SUB):
        @pl.when(jnp.logical_and(live, jnp.logical_and(n_valid > rows - MOE_SUB, n_valid <= rows)))
        def _(rows=rows):
            o_ref[:rows, :] = compute(rows).astype(o_ref.dtype)
            if rows < tm:
                o_ref[rows:, :] = jnp.zeros((tm - rows, o_ref.shape[1]), o_ref.dtype)

    @pl.when(jnp.logical_not(live))
    def _():
        o_ref[...] = jnp.zeros_like(o_ref)


def _gate_up_kernel(be_ref, nb_ref, nx_ref, nv_ref, x_ref, wg_hbm, bg_ref, wu_hbm, bu_ref, o_ref,
                    wf_ref, wgb_ref, wub_ref, sem, *, layer):
    live = _stream_expert_weights(be_ref, nb_ref, nx_ref, (wg_hbm, wu_hbm), wf_ref,
                                  (wgb_ref, wub_ref), sem, layer)

    def compute(rows):
        x = x_ref[:rows, :]
        gate = jnp.minimum(_dot(x, wgb_ref[...]) + bg_ref[0], SWIGLU_LIMIT)
        up = jnp.clip(_dot(x, wub_ref[...]) + bu_ref[0], -SWIGLU_LIMIT, SWIGLU_LIMIT)
        return (up + 1.0) * gate * jax.nn.sigmoid(SWIGLU_ALPHA * gate)

    _for_filled_rows(live, nv_ref[pl.program_id(1)], o_ref, compute)


def _down_kernel(be_ref, nb_ref, nx_ref, nv_ref, a_ref, w_hbm, b_ref, o_ref, wf_ref, wb_ref, sem,
                 *, layer):
    live = _stream_expert_weights(be_ref, nb_ref, nx_ref, (w_hbm,), wf_ref, (wb_ref,), sem, layer)

    def compute(rows):
        return _dot(a_ref[:rows, :], wb_ref[...]) + b_ref[0]

    _for_filled_rows(live, nv_ref[pl.program_id(1)], o_ref, compute)


def expert_ffn(xs, block_expert, n_used, next_expert, n_valid, layer,
               w_gate, b_gate, w_up, b_up, w_down, b_down):
    r, d = xs.shape
    f = w_gate.shape[3]
    nb = r // MOE_TM
    hbm = pl.BlockSpec(memory_space=pl.ANY)
    bspec = lambda n_: pl.BlockSpec((1, 1, n_), lambda j, i, be, nu, nx, nv: (be[i], 0, j))
    rows = lambda w_: pl.BlockSpec((MOE_TM, w_), lambda j, i, be, nu, nx, nv: (i, 0))
    tile = lambda n_: pl.BlockSpec((MOE_TM, n_), lambda j, i, be, nu, nx, nv: (i, j))
    tn = min(f, 512)
    act = pl.pallas_call(
        functools.partial(_gate_up_kernel, layer=layer),
        grid_spec=pltpu.PrefetchScalarGridSpec(
            num_scalar_prefetch=4, grid=(f // tn, nb),
            in_specs=[rows(d), hbm, bspec(tn), hbm, bspec(tn)],
            out_specs=tile(tn),
            scratch_shapes=[pltpu.VMEM((2, d, tn), F32), pltpu.VMEM((d, tn), BF16),
                            pltpu.VMEM((d, tn), BF16), pltpu.SemaphoreType.DMA((2,))]),
        out_shape=jax.ShapeDtypeStruct((r, f), BF16),
        compiler_params=_params("arbitrary", "arbitrary"),
        name="expert_gate_up",
    )(block_expert, n_used, next_expert, n_valid, xs, w_gate, b_gate, w_up, b_up)
    tn = min(d, 1024)
    return pl.pallas_call(
        functools.partial(_down_kernel, layer=layer),
        grid_spec=pltpu.PrefetchScalarGridSpec(
            num_scalar_prefetch=4, grid=(d // tn, nb),
            in_specs=[rows(f), hbm, bspec(tn)],
            out_specs=tile(tn),
            scratch_shapes=[pltpu.VMEM((1, f, tn), F32), pltpu.VMEM((f, tn), BF16),
                            pltpu.SemaphoreType.DMA((1,))]),
        out_shape=jax.ShapeDtypeStruct((r, d), F32),
        compiler_params=_params("arbitrary", "arbitrary"),
        name="expert_down",
    )(block_expert, n_used, next_expert, n_valid, act, w_down, b_down)


COMBINE_TOKENS = 64


def _combine_kernel(pos_ref, ys_ref, x1_ref, rg_ref, gate_ref, lng_ref, lnb_ref, o_ref, buf_ref, sem,
                    *, alpha, tok_offset):
    _, tt, d = x1_ref.shape
    step = pl.program_id(0) * pl.num_programs(1) + pl.program_id(1)
    n_steps = pl.num_programs(0) * pl.num_programs(1)
    slot = step % 2

    def copy(stp, r0, u, sl):
        t = r0 // TOP_K + u // TOP_K
        src_row = pos_ref[(tok_offset + stp * tt) * TOP_K + r0 + u]
        return pltpu.make_async_copy(ys_ref.at[pl.ds(src_row, 1), :],
                                     buf_ref.at[sl, u % TOP_K, pl.ds(t, 1), :], sem.at[sl])

    issue, drain = _row_copy_loops(copy, tt * TOP_K)

    @pl.when(step == 0)
    def _():
        issue(0, 0)

    @pl.when(step + 1 < n_steps)
    def _():
        issue(step + 1, 1 - slot)

    drain(slot)
    rg = rg_ref[0]
    ffn = ((rg[:, 0:1] * buf_ref[slot, 0] + rg[:, 1:2] * buf_ref[slot, 1])
           + (rg[:, 2:3] * buf_ref[slot, 2] + rg[:, 3:4] * buf_ref[slot, 3]))
    y = alpha * x1_ref[0] + gate_ref[0] * ffn
    o_ref[0] = _layer_norm(y, lng_ref[...], lnb_ref[...])


def combine(ys, pos, router_gates, x1, gate, ln_g, ln_b, alpha, tok_offset):
    b, t, d = x1.shape
    tt = min(t, COMBINE_TOKENS)
    row = lambda w_: pl.BlockSpec((1, tt, w_), lambda i, j, p: (i, j, 0))
    mod = pl.BlockSpec((1, 1, d), lambda i, j, p: (i, 0, 0))
    vec = pl.BlockSpec((1, d), lambda i, j, p: (0, 0))
    return pl.pallas_call(
        functools.partial(_combine_kernel, alpha=alpha, tok_offset=tok_offset),
        grid_spec=pltpu.PrefetchScalarGridSpec(
            num_scalar_prefetch=1, grid=(b, t // tt),
            in_specs=[pl.BlockSpec(memory_space=pl.ANY), row(d), row(TOP_K), mod, vec, vec],
            out_specs=row(d),
            scratch_shapes=[pltpu.VMEM((2, TOP_K, tt, d), F32), pltpu.SemaphoreType.DMA((2,))]),
        out_shape=jax.ShapeDtypeStruct((b, t, d), F32),
        compiler_params=pltpu.CompilerParams(dimension_semantics=("arbitrary", "arbitrary"),
                                             vmem_limit_bytes=VMEM_LIMIT, disable_bounds_checks=True),
        name="moe_combine",
    )(pos, ys, x1, router_gates, gate, ln_g.reshape(1, d), ln_b.reshape(1, d))


def _route(logits):
    n_tok, n_exp = logits.shape
    top_logit, top_e = lax.top_k(logits, TOP_K)
    gates = jax.nn.softmax(top_logit, axis=-1)
    nk = n_tok * TOP_K
    flat_e = top_e.reshape(nk).astype(jnp.int32)
    onehot = (flat_e[:, None] == jnp.arange(n_exp, dtype=jnp.int32)[None, :]).astype(jnp.int32)
    running = jnp.cumsum(onehot, axis=0)
    counts = running[-1]
    rank = jnp.sum(running * onehot, axis=1) - 1
    padded = (counts + MOE_TM - 1) // MOE_TM * MOE_TM
    padded_end = jnp.cumsum(padded)
    padded_start = padded_end - padded
    start = jnp.cumsum(counts) - counts
    pos = (padded_start[flat_e] + rank).astype(jnp.int32)
    n_blocks = -(-nk // MOE_TM) + n_exp
    block_start = jnp.arange(n_blocks, dtype=jnp.int32) * MOE_TM
    block_expert = jnp.minimum(jnp.sum(block_start[:, None] >= padded_end[None, :], axis=1),
                               n_exp - 1).astype(jnp.int32)
    n_used = (padded_end[-1:] // MOE_TM).astype(jnp.int32)
    blk = jnp.arange(n_blocks, dtype=jnp.int32)
    run_start = (blk < n_used[0]) & ((blk == 0) | (block_expert != jnp.roll(block_expert, 1)))
    later_start = lax.cummin(jnp.where(run_start, blk, n_blocks)[::-1])[::-1]
    next_start = jnp.concatenate([later_start[1:], jnp.full((1,), n_blocks, jnp.int32)])
    next_expert = jnp.where(next_start < n_blocks,
                            block_expert[jnp.minimum(next_start, n_blocks - 1)], -1).astype(jnp.int32)
    order = jnp.argsort(flat_e).astype(jnp.int32)
    slot = jnp.arange(n_blocks * MOE_TM, dtype=jnp.int32)
    slot_e = jnp.repeat(block_expert, MOE_TM)
    within = slot - padded_start[slot_e]
    valid = within < counts[slot_e]
    src = order[jnp.clip(start[slot_e] + within, 0, nk - 1)]
    slot_tok = jnp.where(valid, src // TOP_K, 0).astype(jnp.int32)
    n_valid = jnp.clip(counts[block_expert] - (block_start - padded_start[block_expert]), 0, MOE_TM)
    n_valid = jnp.where(blk < n_used[0], n_valid, 0).astype(jnp.int32)
    sub_live = valid.reshape(-1, MOE_SUB)[:, 0].astype(jnp.int32)
    return slot_tok, gates, block_expert, n_used, next_expert, n_valid, sub_live, pos


def kernel(x_prompt, x_sample, c_prompt, c_sample, cache_sb_k, cache_sb_v, state_hgrn,
           w_ada, b_ada, w_in, w_out, norm_a, norm_b, lb_logits,
           ln1_g, ln1_b, ln2_g, ln2_b, router_w, router_b,
           w_gate, b_gate, w_up, b_up, w_down, b_down):
    depth = w_ada.shape[0]
    d = x_prompt.shape[-1]
    alpha = (2.0 * depth) ** 0.25
    d_sb = cache_sb_k.shape[2] * HEAD_DIM
    hg_heads = state_hgrn.shape[2]
    d_hk = hg_heads * state_hgrn.shape[3]
    d_hg = hg_heads * state_hgrn.shape[4]
    sb_scale = HEAD_DIM ** -0.5
    n_exp = router_w.shape[-1]
    lower_bounds = jnp.cumsum(jax.nn.softmax(lb_logits.astype(F32), axis=0), axis=0)
    streams = [x_prompt, x_sample]
    conds = [c_prompt, c_sample]
    n_rows = [c.shape[0] for c in conds]
    pad = (-sum(n_rows)) % 8
    c_all = jnp.concatenate(conds + [jnp.zeros((pad, d), F32)], axis=0)
    outs = [[] for _ in range(6)]

    for l in range(depth):
        mods = ada_modulation(c_all, w_ada[l], b_ada[l])
        w_in_l = w_in[l]
        seg = [0, d_sb, 2 * d_sb, 3 * d_sb, 3 * d_sb + d_hk, 3 * d_sb + 2 * d_hk,
               3 * d_sb + 2 * d_hk + d_hg, 3 * d_sb + 2 * d_hk + 2 * d_hg]
        w_seg = [w_in_l[:, seg[i]:seg[i + 1]].astype(BF16) for i in range(7)]
        wo_a = w_out[l, :d_sb].astype(BF16)
        wo_b = w_out[l, d_sb:].astype(BF16)
        gain_a = norm_a[l].reshape(-1, 1, HEAD_DIM)
        gain_b = norm_b[l].reshape(-1, 1, HEAD_DIM)
        lb = lower_bounds[l].reshape(hg_heads, 1, HEAD_DIM)
        x1s, h2s, lgs, gate2s = [], [], [], []
        row0 = 0
        for si, x in enumerate(streams):
            b = x.shape[0]
            m = mods[row0:row0 + b].reshape(b, N_MOD, 1, d)
            row0 += b
            shift1, scale1, gate1, shift2, scale2, gate2 = [m[:, i] for i in range(N_MOD)]
            h = modulate(x, scale1, shift1)
            (q_a,) = project_heads(h, w_seg[0], [BF16], scale=sb_scale)
            k_f, k_h = project_heads(h, w_seg[1], [F32, BF16])
            v_f, v_h = project_heads(h, w_seg[2], [F32, BF16])
            (q_b,) = project_heads(h, w_seg[3], [F32])
            (f_b,) = project_heads(h, w_seg[4], [F32])
            (i_b,) = project_heads(h, w_seg[5], [F32])
            (g_b,) = project_heads(h, w_seg[6], [F32])
            if si == 0:
                o_a = stick_breaking(q_a, k_h, v_h, gain_a)
                s0 = jnp.zeros((b, hg_heads, d_hk // hg_heads, HEAD_DIM), F32)
            else:
                o_a = stick_breaking(q_a, k_h, v_h, gain_a, cache_sb_k[l], cache_sb_v[l])
                s0 = state_hgrn[l]
            o_b, s_new = hgrn2(q_b, f_b, i_b, g_b, lb, s0, gain_b)
            x1, h2, lg = out_projection(o_a, o_b, wo_a, wo_b, x, gate1, scale2, shift2,
                                        ln1_g[l], ln1_b[l], router_w[l], router_b[l], alpha)
            x1s.append(x1)
            h2s.append(h2.reshape(-1, d))
            lgs.append(lg.reshape(-1, n_exp))
            gate2s.append(gate2)
            outs[3 * si + 0].append(k_f)
            outs[3 * si + 1].append(v_f)
            outs[3 * si + 2].append(s_new)

        (slot_tok, router_gates, block_expert, n_used, next_expert, n_valid, sub_live,
         pos) = _route(jnp.concatenate(lgs, axis=0))
        xs = gather_rows(jnp.concatenate(h2s, axis=0), slot_tok, sub_live)
        ys = expert_ffn(xs, block_expert, n_used, next_expert, n_valid, l,
                        w_gate, b_gate[l][:, None, :], w_up, b_up[l][:, None, :],
                        w_down, b_down[l][:, None, :])
        tok0 = 0
        new_streams = []
        for si, x1 in enumerate(x1s):
            n_s = x1.shape[0] * x1.shape[1]
            rg = router_gates[tok0:tok0 + n_s].reshape(x1.shape[0], x1.shape[1], TOP_K)
            new_streams.append(combine(ys, pos, rg, x1, gate2s[si], ln2_g[l], ln2_b[l], alpha, tok0))
            tok0 += n_s
        streams = new_streams

    stack = lambda xs_: jnp.stack(xs_)
    return (streams[0], streams[1], stack(outs[0]), stack(outs[1]), stack(outs[2]),
            stack(outs[3]), stack(outs[4]), stack(outs[5]))
```

```python
import functools
import math

import jax
import jax.numpy as jnp
from jax import lax
from jax.experimental import pallas as pl
from jax.experimental.pallas import tpu as pltpu

HEAD_DIM = 128
TOP_K = 4
SWIGLU_LIMIT = 7.0
SWIGLU_ALPHA = 1.702
LN_EPS = 1e-5
RMS_EPS = 1e-6
N_MOD = 6

LANES = 128
PROJ_ROWS = 1024
SB_TILE = 256
SB_HEADS_PROMPT = 2
SB_HEADS_CACHED = 4
SB_DEAD = -104.0
HG_GROUP = 16
HG_UNROLL = 8
HG_CHUNK = 64
HG_CHUNK_UNROLL = 4
HG_SAFE = 60.0
MOE_TM = 1024
MOE_SUB = 256
CAST_ROWS = 64
VMEM_LIMIT = 56 * 1024 * 1024

F32 = jnp.float32
BF16 = jnp.bfloat16


def _params(*sem):
    return pltpu.CompilerParams(dimension_semantics=sem, vmem_limit_bytes=VMEM_LIMIT)


def _dot(a, b):
    return jnp.dot(a, b, preferred_element_type=F32)


def _dot_nt(a, b):
    return lax.dot_general(a, b, (((1,), (1,)), ((), ())), preferred_element_type=F32)


def _dot_tn(a, b):
    return lax.dot_general(a, b, (((0,), (0,)), ((), ())), preferred_element_type=F32)


def _split3(x):
    hi = x.astype(BF16)
    r1 = x - hi.astype(F32)
    mid = r1.astype(BF16)
    lo = (r1 - mid.astype(F32)).astype(BF16)
    return hi, mid, lo


def _dot_exact_rhs(x, m):
    hi, mid, lo = _split3(x)
    return _dot(m, hi) + _dot(m, mid) + _dot(m, lo)


def _softplus(z):
    return jnp.maximum(z, 0.0) + jnp.log(1.0 + jnp.exp(-jnp.abs(z)))


def _ada_kernel(c_ref, w_ref, b_ref, o_ref):
    c = c_ref[...]
    s = (c * jax.nn.sigmoid(c)).astype(BF16)
    o_ref[...] = _dot(s, w_ref[...].astype(BF16)) + b_ref[...]


def ada_modulation(c, w, b):
    r, d = c.shape
    n = w.shape[1]
    tn = min(n, 512)
    return pl.pallas_call(
        _ada_kernel,
        grid=(n // tn,),
        in_specs=[pl.BlockSpec((r, d), lambda j: (0, 0)),
                  pl.BlockSpec((d, tn), lambda j: (0, j)),
                  pl.BlockSpec((1, tn), lambda j: (0, j))],
        out_specs=pl.BlockSpec((r, tn), lambda j: (0, j)),
        out_shape=jax.ShapeDtypeStruct((r, n), F32),
        compiler_params=_params("arbitrary"),
        name="ada_modulation",
    )(c, w, b.reshape(1, n))


def _modulate_kernel(x_ref, sc_ref, sh_ref, o_ref):
    o_ref[...] = (x_ref[...] * (1.0 + sc_ref[...]) + sh_ref[...]).astype(o_ref.dtype)


def _row_tiles(b, t, long_rows=512, short_rows=256):
    tt = min(t, long_rows)
    bb = max(1, min(b, short_rows // tt)) if tt < short_rows else 1
    return bb, tt


def modulate(x, scale, shift):
    b, t, d = x.shape
    bb, tt = _row_tiles(b, t)
    mod_spec = pl.BlockSpec((bb, 1, d), lambda i, j: (i, 0, 0))
    return pl.pallas_call(
        _modulate_kernel,
        grid=(b // bb, t // tt),
        in_specs=[pl.BlockSpec((bb, tt, d), lambda i, j: (i, j, 0)), mod_spec, mod_spec],
        out_specs=pl.BlockSpec((bb, tt, d), lambda i, j: (i, j, 0)),
        out_shape=jax.ShapeDtypeStruct((b, t, d), BF16),
        compiler_params=_params("parallel", "parallel"),
        name="modulate",
    )(x, scale, shift)


def _proj_kernel(h_ref, w_ref, *o_refs, scale):
    bb, tt, d = h_ref.shape
    acc = _dot(h_ref[...].reshape(bb * tt, d), w_ref[...])
    if scale != 1.0:
        acc = acc * scale
    heads = o_refs[0].shape[1]
    for hh in range(heads):
        blk = acc[:, hh * HEAD_DIM:(hh + 1) * HEAD_DIM].reshape(bb, tt, HEAD_DIM)
        for o_ref in o_refs:
            o_ref[:, hh] = blk.astype(o_ref.dtype)


def project_heads(h, w, out_dtypes, scale=1.0):
    b, t, d = h.shape
    n = w.shape[1]
    heads = n // HEAD_DIM
    bb, tt = _row_tiles(b, t, PROJ_ROWS, PROJ_ROWS)
    tn = min(n, 1024)
    hpb = tn // HEAD_DIM
    out_spec = pl.BlockSpec((bb, hpb, tt, HEAD_DIM), lambda i, j, k: (i, k, j, 0))
    outs = pl.pallas_call(
        functools.partial(_proj_kernel, scale=scale),
        grid=(b // bb, t // tt, n // tn),
        in_specs=[pl.BlockSpec((bb, tt, d), lambda i, j, k: (i, j, 0)),
                  pl.BlockSpec((d, tn), lambda i, j, k: (0, k))],
        out_specs=[out_spec] * len(out_dtypes),
        out_shape=[jax.ShapeDtypeStruct((b, heads, t, HEAD_DIM), dt) for dt in out_dtypes],
        compiler_params=_params("parallel", "parallel", "arbitrary"),
        name="project_heads",
    )(h, w)
    return outs


def _sb_block(q, k, v, u_ext, carry, row0):
    tk = k.shape[0]
    masked = row0 is not None
    z = _dot_nt(q, k)
    lf = -_softplus(z)
    if masked:
        row = lax.broadcasted_iota(jnp.int32, z.shape, 0) + row0
        col = lax.broadcasted_iota(jnp.int32, z.shape, 1)
        earlier = col < row
        lf = jnp.where(earlier, lf, 0.0)
    hi = lf.astype(BF16)
    lo = (lf - hi.astype(F32)).astype(BF16)
    cs = _dot(hi, u_ext) + _dot(lo, u_ext)
    if tk >= LANES:
        later = jnp.concatenate([carry] * (tk // LANES), axis=1)
    else:
        later = carry[:, :tk]
    after = cs[:, LANES:] + later
    w = jnp.exp(z + lf + after)
    if masked:
        w = jnp.where(earlier, w, 0.0)
    out = _dot(w.astype(BF16), v)
    return out, carry + cs[:, :LANES]


def _sb_kernel(q_ref, kd_ref, vd_ref, kp_ref, vp_ref, ud_ref, up_ref, gain_ref, o_ref,
               acc_ref, carry_ref, *, past_blocks):
    heads, tq = q_ref.shape[1], q_ref.shape[2]
    tk = up_ref.shape[0]
    n_past = pl.program_id(2) * (tq // tk) if past_blocks is None else past_blocks

    def own_keys(hd):
        return _sb_block(q_ref[0, hd], kd_ref[0, hd].astype(BF16), vd_ref[0, hd].astype(BF16),
                         ud_ref[...], jnp.zeros((tq, LANES), F32), row0=0)

    def past_block(hd, kb, carry):
        start = pl.multiple_of(kb * tk, tk)
        k = kp_ref[0, hd, pl.ds(start, tk), :].astype(BF16)
        v = vp_ref[0, hd, pl.ds(start, tk), :].astype(BF16)
        return _sb_block(q_ref[0, hd], k, v, up_ref[...], carry, row0=None)

    def own_and_nearest():
        for hd in range(heads):
            out_d, carry_d = own_keys(hd)
            out_p, carry_p = past_block(hd, n_past - 1, carry_d)
            acc_ref[hd] = out_d + out_p
            carry_ref[hd] = carry_p

    def own_only():
        for hd in range(heads):
            out_d, carry_d = own_keys(hd)
            acc_ref[hd] = out_d
            carry_ref[hd] = carry_d

    if past_blocks is None:
        pl.when(n_past > 0)(own_and_nearest)
        pl.when(n_past == 0)(own_only)
    elif past_blocks > 0:
        own_and_nearest()
    else:
        own_only()

    def cond(state):
        kb, alive = state
        return jnp.logical_and(kb >= 0, alive)

    def body(state):
        kb, _ = state
        for hd in range(heads):
            out, carry = past_block(hd, kb, carry_ref[hd])
            acc_ref[hd] += out
            carry_ref[hd] = carry
        return kb - 1, jnp.max(carry_ref[...]) > SB_DEAD

    lax.while_loop(cond, body, (n_past - 2, jnp.max(carry_ref[...]) > SB_DEAD))
    for hd in range(heads):
        o = acc_ref[hd]
        o = o * lax.rsqrt(jnp.mean(o * o, axis=-1, keepdims=True) + RMS_EPS)
        o_ref[0, :, hd * HEAD_DIM:(hd + 1) * HEAD_DIM] = (o * gain_ref[hd]).astype(o_ref.dtype)


def _suffix_matrix(tk):
    j = jnp.arange(tk)[:, None]
    c = jnp.arange(LANES + tk)[None, :]
    return jnp.where((c < LANES) | (j > c - LANES), 1.0, 0.0).astype(BF16)


def stick_breaking(q, k, v, gain, past_k=None, past_v=None):
    b, h, t, d = q.shape
    if past_k is None:
        tq = min(t, SB_TILE)
        tk = tq
        kp, vp, past_blocks = k, v, None
        hb = math.gcd(h, SB_HEADS_PROMPT)
    else:
        tq = t
        tk = min(past_k.shape[2], SB_TILE)
        kp, vp, past_blocks = past_k, past_v, past_k.shape[2] // tk
        hb = math.gcd(h, SB_HEADS_CACHED)
    p = kp.shape[2]
    tile_spec = pl.BlockSpec((1, hb, tq, d), lambda i, j, m: (i, j, m, 0))
    past_spec = pl.BlockSpec((1, hb, p, d), lambda i, j, m: (i, j, 0, 0))
    return pl.pallas_call(
        functools.partial(_sb_kernel, past_blocks=past_blocks),
        grid=(b, h // hb, t // tq),
        in_specs=[tile_spec, tile_spec, tile_spec, past_spec, past_spec,
                  pl.BlockSpec((tq, tq + LANES), lambda i, j, m: (0, 0)),
                  pl.BlockSpec((tk, tk + LANES), lambda i, j, m: (0, 0)),
                  pl.BlockSpec((hb, 1, d), lambda i, j, m: (j, 0, 0))],
        out_specs=pl.BlockSpec((1, tq, hb * d), lambda i, j, m: (i, m, j)),
        out_shape=jax.ShapeDtypeStruct((b, t, h * d), BF16),
        scratch_shapes=[pltpu.VMEM((hb, tq, LANES), F32), pltpu.VMEM((hb, tq, LANES), F32)],
        compiler_params=_params("parallel", "parallel", "arbitrary"),
        name="stick_breaking",
    )(q, k, v, kp, vp, _suffix_matrix(tq), _suffix_matrix(tk), gain)


def _blockwise_prefix(x, m):
    rows = m.shape[0]
    parts = [_dot_exact_rhs(x[r * rows:(r + 1) * rows], m) for r in range(x.shape[0] // rows)]
    return jnp.concatenate(parts, axis=0) if len(parts) > 1 else parts[0]


def _hgrn_kernel(q_ref, a_ref, i_ref, g_ref, lb_ref, s0_ref, gain_ref, cum_ref, tot_ref, ones_ref,
                 ccum_ref, ctot_ref, chalf_ref,
                 o_ref, s_ref, st_ref, qd_ref, kd_ref, vb_ref, dec_ref, od_ref, oi_ref):
    tt = q_ref.shape[2]
    ti = pl.program_id(2)

    @pl.when(ti == 0)
    def _():
        st_ref[...] = s0_ref[0, 0].T

    lb = lb_ref[0]
    a = a_ref[0, 0]
    log_sig = -_softplus(-a)
    x = jnp.log(lb)
    y = jnp.log1p(-lb) + log_sig
    log_f = jnp.maximum(x, y) + jnp.log(1.0 + jnp.exp(-jnp.abs(x - y)))
    kk = (1.0 - lb) * jax.nn.sigmoid(-a)
    q = q_ref[0, 0]
    v = i_ref[0, 0]

    vb_ref[...] = v.astype(BF16)

    b_half = _blockwise_prefix(log_f, chalf_ref[...])
    b_end = _blockwise_prefix(log_f, ctot_ref[...])
    safe = jnp.logical_and(jnp.min(b_half) > -HG_SAFE, jnp.min(b_end - b_half) > -HG_SAFE)

    def recurrence(step_rows, unroll):
        def step(i, st):
            r0 = pl.multiple_of(i * step_rows, step_rows)
            rows_i = pl.ds(r0, step_rows)
            oi_ref[rows_i, :] = _dot_nt(qd_ref[rows_i, :], st.astype(BF16))
            upd = _dot_tn(vb_ref[rows_i, :], kd_ref[rows_i, :])
            return st * dec_ref[pl.ds(r0, 1), :] + upd
        n = tt // step_rows
        st_ref[...] = lax.fori_loop(0, n, step, st_ref[...], unroll=min(n, unroll))

    @pl.when(safe)
    def _():
        ccum = ccum_ref[...]
        b_c = _blockwise_prefix(log_f, ccum)
        qd_ref[...] = (q * jnp.exp(b_c)).astype(BF16)
        kd_ref[...] = (kk * jnp.exp(b_end - b_c)).astype(BF16)
        dec_ref[...] = jnp.exp(b_end)
        qm = (q * jnp.exp(b_c - b_half)).astype(BF16)
        km = (kk * jnp.exp(b_half - b_c)).astype(BF16)
        same_chunk_earlier = ccum > 0
        rows = ccum.shape[0]
        for r in range(tt // rows):
            sl = slice(r * rows, (r + 1) * rows)
            sc = jnp.where(same_chunk_earlier, _dot_nt(qm[sl], km[sl]), 0.0)
            od_ref[sl, :] = _dot(sc.astype(BF16), vb_ref[sl, :])
        recurrence(min(tt, HG_CHUNK), HG_CHUNK_UNROLL)

    @pl.when(jnp.logical_not(safe))
    def _():
        ng = tt // HG_GROUP
        bcum = _blockwise_prefix(log_f, cum_ref[...])
        bend = _blockwise_prefix(log_f, tot_ref[...])
        qd_ref[...] = (q * jnp.exp(bcum)).astype(BF16)
        kd_ref[...] = (kk * jnp.exp(bend - bcum)).astype(BF16)
        dec_ref[...] = jnp.exp(bend)
        b3 = bcum.reshape(ng, HG_GROUP, HEAD_DIM)
        q3 = q.reshape(ng, HG_GROUP, HEAD_DIM)
        k3 = kk.reshape(ng, HG_GROUP, HEAD_DIM)
        v3 = v.reshape(ng, HG_GROUP, HEAD_DIM)
        pos = lax.broadcasted_iota(jnp.int32, (ng, HG_GROUP, HEAD_DIM), 1)
        ones = ones_ref[...]
        od = jnp.zeros((tt, HEAD_DIM), F32)
        for s in range(HG_GROUP):
            diff = b3 - b3[:, s:s + 1, :]
            wgt = jnp.where(pos >= s, jnp.exp(jnp.minimum(diff, 0.0)), 0.0)
            prod = (q3 * k3[:, s:s + 1, :] * wgt).reshape(tt, HEAD_DIM)
            score = _dot(prod.astype(BF16), ones)
            vs = jnp.broadcast_to(v3[:, s:s + 1, :], (ng, HG_GROUP, HEAD_DIM)).reshape(tt, HEAD_DIM)
            od = od + score * vs
        od_ref[...] = od
        recurrence(HG_GROUP, HG_UNROLL)

    o = od_ref[...] + oi_ref[...]
    o = o * lax.rsqrt(jnp.mean(o * o, axis=-1, keepdims=True) + RMS_EPS) * gain_ref[0]
    gg = g_ref[0, 0]
    o_ref[0] = (o * (gg * jax.nn.sigmoid(gg))).astype(o_ref.dtype)

    @pl.when(ti == pl.num_programs(2) - 1)
    def _():
        s_ref[0, 0] = st_ref[...].T


def _span_matrices(rows, span):
    r = jnp.arange(rows)
    same = (r[:, None] // span) == (r[None, :] // span)
    cum = jnp.where(same & (r[None, :] <= r[:, None]), 1.0, 0.0).astype(BF16)
    tot = jnp.where(same, 1.0, 0.0).astype(BF16)
    half = jnp.where(same & ((r[None, :] % span) < span // 2), 1.0, 0.0).astype(BF16)
    return cum, tot, half


def hgrn2(q, a, i, g, lb, s0, gain):
    b, h, t, d = q.shape
    tt = min(t, 1024)
    rows = min(tt, LANES)
    cum, tot, _ = _span_matrices(rows, HG_GROUP)
    ccum, ctot, chalf = _span_matrices(rows, min(tt, HG_CHUNK))
    ones = jnp.ones((d, d), BF16)
    seq_spec = pl.BlockSpec((1, 1, tt, d), lambda bi, hi, ti: (bi, hi, ti, 0))
    head_spec = pl.BlockSpec((1, 1, d), lambda bi, hi, ti: (hi, 0, 0))
    state_spec = pl.BlockSpec((1, 1, d, d), lambda bi, hi, ti: (bi, hi, 0, 0))
    const = lambda shape: pl.BlockSpec(shape, lambda bi, hi, ti: (0, 0))
    return pl.pallas_call(
        _hgrn_kernel,
        grid=(b, h, t // tt),
        in_specs=[seq_spec, seq_spec, seq_spec, seq_spec, head_spec, state_spec, head_spec,
                  const((rows, rows)), const((rows, rows)), const((d, d)),
                  const((rows, rows)), const((rows, rows)), const((rows, rows))],
        out_specs=[pl.BlockSpec((1, tt, d), lambda bi, hi, ti: (bi, ti, hi)), state_spec],
        out_shape=[jax.ShapeDtypeStruct((b, t, h * d), BF16),
                   jax.ShapeDtypeStruct((b, h, d, d), F32)],
        scratch_shapes=([pltpu.VMEM((d, d), F32)] + [pltpu.VMEM((tt, d), BF16)] * 3
                        + [pltpu.VMEM((tt, d), F32)] * 3),
        compiler_params=_params("parallel", "parallel", "arbitrary"),
        name="hgrn2",
    )(q, a, i, g, lb, s0, gain, cum, tot, ones, ccum, ctot, chalf)


def _layer_norm(x, g, b):
    mu = jnp.mean(x, axis=-1, keepdims=True)
    xc = x - mu
    var = jnp.mean(xc * xc, axis=-1, keepdims=True)
    return xc * lax.rsqrt(var + LN_EPS) * g + b


def _out_kernel(oa_ref, ob_ref, wa_ref, wb_ref, x_ref, gate_ref, sc_ref, sh_ref, lng_ref, lnb_ref,
                rw_ref, rb_ref, x1_ref, h2_ref, lg_ref, acc_ref, *, alpha):
    bb, tt, dh = oa_ref.shape
    tn = wa_ref.shape[1]
    n = pl.program_id(2)
    part = (_dot(oa_ref[...].reshape(bb * tt, dh), wa_ref[...])
            + _dot(ob_ref[...].reshape(bb * tt, dh), wb_ref[...]))
    acc_ref[n] = part

    @pl.when(n == pl.num_programs(2) - 1)
    def _():
        d = x_ref.shape[2]
        mix = jnp.concatenate([acc_ref[k] for k in range(d // tn)], axis=1).reshape(bb, tt, d)
        x1 = _layer_norm(alpha * x_ref[...] + gate_ref[...] * mix, lng_ref[...], lnb_ref[...])
        x1_ref[...] = x1
        h2 = x1 * (1.0 + sc_ref[...]) + sh_ref[...]
        h2_ref[...] = h2.astype(h2_ref.dtype)
        hh, hm, hl = _split3(h2.reshape(bb * tt, d))
        w = rw_ref[...]
        wh = w.astype(BF16)
        wl = (w - wh.astype(F32)).astype(BF16)
        lg = (_dot(hh, wh) + (_dot(hm, wh) + _dot(hh, wl))) + (_dot(hl, wh) + _dot(hm, wl))
        lg_ref[...] = (lg + rb_ref[...]).reshape(bb, tt, lg.shape[-1])


def out_projection(oa, ob, wa, wb, x, gate, scale, shift, ln_g, ln_b, router_w, router_b, alpha):
    b, t, dh = oa.shape
    d = x.shape[2]
    e = router_w.shape[1]
    bb, tt = _row_tiles(b, t)
    tt = min(tt, 256)
    tn = min(d, 512)
    row = lambda w_: pl.BlockSpec((bb, tt, w_), lambda i, j, n: (i, j, 0))
    mod = pl.BlockSpec((bb, 1, d), lambda i, j, n: (i, 0, 0))
    vec = lambda w_: pl.BlockSpec((1, w_), lambda i, j, n: (0, 0))
    wspec = pl.BlockSpec((dh, tn), lambda i, j, n: (0, n))
    return pl.pallas_call(
        functools.partial(_out_kernel, alpha=alpha),
        grid=(b // bb, t // tt, d // tn),
        in_specs=[row(dh), row(dh), wspec, wspec, row(d), mod, mod, mod, vec(d), vec(d),
                  pl.BlockSpec((d, e), lambda i, j, n: (0, 0)), vec(e)],
        out_specs=[row(d), row(d), row(e)],
        out_shape=[jax.ShapeDtypeStruct((b, t, d), F32), jax.ShapeDtypeStruct((b, t, d), F32),
                   jax.ShapeDtypeStruct((b, t, e), F32)],
        scratch_shapes=[pltpu.VMEM((d // tn, bb * tt, tn), F32)],
        compiler_params=_params("parallel", "parallel", "arbitrary"),
        name="out_projection",
    )(oa, ob, wa, wb, x, gate, scale, shift, ln_g.reshape(1, d), ln_b.reshape(1, d),
      router_w, router_b.reshape(1, e))


DMA_UNROLL = 8


def _row_copy_loops(copy, n):
    def issue(blk, slot):
        def body(g, c):
            r0 = pl.multiple_of(g * DMA_UNROLL, DMA_UNROLL)
            for u in range(DMA_UNROLL):
                copy(blk, r0, u, slot).start()
            return c
        lax.fori_loop(0, n // DMA_UNROLL, body, 0)

    def drain(slot):
        def body(g, c):
            for _ in range(DMA_UNROLL):
                copy(0, 0, 0, slot).wait()
            return c
        lax.fori_loop(0, n // DMA_UNROLL, body, 0)

    return issue, drain


def _gather_kernel(tok_ref, live_ref, src_ref, o_ref, buf_ref, sem):
    i = pl.program_id(0)
    slot = i % 2

    def copy(blk, r0, u, sl):
        return pltpu.make_async_copy(src_ref.at[pl.ds(tok_ref[blk * MOE_SUB + r0 + u], 1), :],
                                     buf_ref.at[sl, pl.ds(r0 + u, 1), :], sem.at[sl])

    issue, drain = _row_copy_loops(copy, MOE_SUB)

    @pl.when(jnp.logical_and(i == 0, live_ref[0] > 0))
    def _():
        issue(0, 0)

    nxt = jnp.minimum(i + 1, pl.num_programs(0) - 1)

    @pl.when(jnp.logical_and(i + 1 < pl.num_programs(0), live_ref[nxt] > 0))
    def _():
        issue(i + 1, 1 - slot)

    @pl.when(live_ref[i] > 0)
    def _():
        drain(slot)
        o_ref[...] = buf_ref[slot].astype(o_ref.dtype)

    @pl.when(live_ref[i] == 0)
    def _():
        o_ref[...] = jnp.zeros_like(o_ref)


def gather_rows(src, tok, sub_live):
    r = tok.shape[0]
    d = src.shape[1]
    return pl.pallas_call(
        _gather_kernel,
        grid_spec=pltpu.PrefetchScalarGridSpec(
            num_scalar_prefetch=2, grid=(r // MOE_SUB,),
            in_specs=[pl.BlockSpec(memory_space=pl.ANY)],
            out_specs=pl.BlockSpec((MOE_SUB, d), lambda i, tok_, lv_: (i, 0)),
            scratch_shapes=[pltpu.VMEM((2, MOE_SUB, d), F32), pltpu.SemaphoreType.DMA((2,))]),
        out_shape=jax.ShapeDtypeStruct((r, d), BF16),
        compiler_params=pltpu.CompilerParams(dimension_semantics=("arbitrary",),
                                             vmem_limit_bytes=VMEM_LIMIT, disable_bounds_checks=True),
        name="gather_rows",
    )(tok, sub_live, src)


def _expert_changed(be_ref, i):
    return jnp.logical_or(i == 0, be_ref[i] != be_ref[jnp.maximum(i - 1, 0)])


def _stream_expert_weights(be_ref, nb_ref, nx_ref, w_hbms, wf_ref, wb_refs, sem, layer):
    j, i = pl.program_id(0), pl.program_id(1)
    tn = wf_ref.shape[2]
    live = i < nb_ref[0]

    def copies(e):
        cols = pl.ds(pl.multiple_of(j * tn, tn), tn)
        return [pltpu.make_async_copy(w.at[layer, e, :, cols], wf_ref.at[m], sem.at[m])
                for m, w in enumerate(w_hbms)]

    @pl.when(jnp.logical_and(live, i == 0))
    def _():
        for c in copies(be_ref[0]):
            c.start()

    @pl.when(jnp.logical_and(live, _expert_changed(be_ref, i)))
    def _():
        for c in copies(be_ref[i]):
            c.wait()
        def convert(c, carry):
            rows = pl.ds(pl.multiple_of(c * CAST_ROWS, CAST_ROWS), CAST_ROWS)
            for m, wb_ref in enumerate(wb_refs):
                wb_ref[rows, :] = wf_ref[m, rows, :].astype(BF16)
            return carry

        lax.fori_loop(0, wf_ref.shape[1] // CAST_ROWS, convert, 0)
        nxt = nx_ref[i]

        @pl.when(nxt >= 0)
        def _():
            for c in copies(nxt):
                c.start()

    return live


def _for_filled_rows(live, n_valid, o_ref, compute):
    tm = o_ref.shape[0]
    for rows in range(MOE_SUB, tm + 1, MOE_SUB):
        @pl.when(jnp.logical_and(live, jnp.logical_and(n_valid > rows - MOE_SUB, n_valid <= rows)))
        def _(rows=rows):
            o_ref[:rows, :] = compute(rows).astype(o_ref.dtype)
            if rows < tm:
                o_ref[rows:, :] = jnp.zeros((tm - rows, o_ref.shape[1]), o_ref.dtype)

    @pl.when(jnp.logical_not(live))
    def _():
        o_ref[...] = jnp.zeros_like(o_ref)


def _gate_up_kernel(be_ref, nb_ref, nx_ref, nv_ref, x_ref, wg_hbm, bg_ref, wu_hbm, bu_ref, o_ref,
                    wf_ref, wgb_ref, wub_ref, sem, *, layer):
    live = _stream_expert_weights(be_ref, nb_ref, nx_ref, (wg_hbm, wu_hbm), wf_ref,
                                  (wgb_ref, wub_ref), sem, layer)

    def compute(rows):
        x = x_ref[:rows, :]
        gate = jnp.minimum(_dot(x, wgb_ref[...]) + bg_ref[0], SWIGLU_LIMIT)
        up = jnp.clip(_dot(x, wub_ref[...]) + bu_ref[0], -SWIGLU_LIMIT, SWIGLU_LIMIT)
        return (up + 1.0) * gate * jax.nn.sigmoid(SWIGLU_ALPHA * gate)

    _for_filled_rows(live, nv_ref[pl.program_id(1)], o_ref, compute)


def _down_kernel(be_ref, nb_ref, nx_ref, nv_ref, a_ref, w_hbm, b_ref, o_ref, wf_ref, wb_ref, sem,
                 *, layer):
    live = _stream_expert_weights(be_ref, nb_ref, nx_ref, (w_hbm,), wf_ref, (wb_ref,), sem, layer)

    def compute(rows):
        return _dot(a_ref[:rows, :], wb_ref[...]) + b_ref[0]

    _for_filled_rows(live, nv_ref[pl.program_id(1)], o_ref, compute)


def expert_ffn(xs, block_expert, n_used, next_expert, n_valid, layer,
               w_gate, b_gate, w_up, b_up, w_down, b_down):
    r, d = xs.shape
    f = w_gate.shape[3]
    nb = r // MOE_TM
    hbm = pl.BlockSpec(memory_space=pl.ANY)
    bspec = lambda n_: pl.BlockSpec((1, 1, n_), lambda j, i, be, nu, nx, nv: (be[i], 0, j))
    rows = lambda w_: pl.BlockSpec((MOE_TM, w_), lambda j, i, be, nu, nx, nv: (i, 0))
    tile = lambda n_: pl.BlockSpec((MOE_TM, n_), lambda j, i, be, nu, nx, nv: (i, j))
    tn = min(f, 512)
    act = pl.pallas_call(
        functools.partial(_gate_up_kernel, layer=layer),
        grid_spec=pltpu.PrefetchScalarGridSpec(
            num_scalar_prefetch=4, grid=(f // tn, nb),
            in_specs=[rows(d), hbm, bspec(tn), hbm, bspec(tn)],
            out_specs=tile(tn),
            scratch_shapes=[pltpu.VMEM((2, d, tn), F32), pltpu.VMEM((d, tn), BF16),
                            pltpu.VMEM((d, tn), BF16), pltpu.SemaphoreType.DMA((2,))]),
        out_shape=jax.ShapeDtypeStruct((r, f), BF16),
        compiler_params=_params("arbitrary", "arbitrary"),
        name="expert_gate_up",
    )(block_expert, n_used, next_expert, n_valid, xs, w_gate, b_gate, w_up, b_up)
    tn = min(d, 1024)
    return pl.pallas_call(
        functools.partial(_down_kernel, layer=layer),
        grid_spec=pltpu.PrefetchScalarGridSpec(
            num_scalar_prefetch=4, grid=(d // tn, nb),
            in_specs=[rows(f), hbm, bspec(tn)],
            out_specs=tile(tn),
            scratch_shapes=[pltpu.VMEM((1, f, tn), F32), pltpu.VMEM((f, tn), BF16),
                            pltpu.SemaphoreType.DMA((1,))]),
        out_shape=jax.ShapeDtypeStruct((r, d), F32),
        compiler_params=_params("arbitrary", "arbitrary"),
        name="expert_down",
    )(block_expert, n_used, next_expert, n_valid, act, w_down, b_down)


COMBINE_TOKENS = 64


def _combine_kernel(pos_ref, ys_ref, x1_ref, rg_ref, gate_ref, lng_ref, lnb_ref, o_ref, buf_ref, sem,
                    *, alpha, tok_offset):
    _, tt, d = x1_ref.shape
    step = pl.program_id(0) * pl.num_programs(1) + pl.program_id(1)
    n_steps = pl.num_programs(0) * pl.num_programs(1)
    slot = step % 2

    def copy(stp, r0, u, sl):
        t = r0 // TOP_K + u // TOP_K
        src_row = pos_ref[(tok_offset + stp * tt) * TOP_K + r0 + u]
        return pltpu.make_async_copy(ys_ref.at[pl.ds(src_row, 1), :],
                                     buf_ref.at[sl, u % TOP_K, pl.ds(t, 1), :], sem.at[sl])

    issue, drain = _row_copy_loops(copy, tt * TOP_K)

    @pl.when(step == 0)
    def _():
        issue(0, 0)

    @pl.when(step + 1 < n_steps)
    def _():
        issue(step + 1, 1 - slot)

    drain(slot)
    rg = rg_ref[0]
    ffn = ((rg[:, 0:1] * buf_ref[slot, 0] + rg[:, 1:2] * buf_ref[slot, 1])
           + (rg[:, 2:3] * buf_ref[slot, 2] + rg[:, 3:4] * buf_ref[slot, 3]))
    y = alpha * x1_ref[0] + gate_ref[0] * ffn
    o_ref[0] = _layer_norm(y, lng_ref[...], lnb_ref[...])


def combine(ys, pos, router_gates, x1, gate, ln_g, ln_b, alpha, tok_offset):
    b, t, d = x1.shape
    tt = min(t, COMBINE_TOKENS)
    row = lambda w_: pl.BlockSpec((1, tt, w_), lambda i, j, p: (i, j, 0))
    mod = pl.BlockSpec((1, 1, d), lambda i, j, p: (i, 0, 0))
    vec = pl.BlockSpec((1, d), lambda i, j, p: (0, 0))
    return pl.pallas_call(
        functools.partial(_combine_kernel, alpha=alpha, tok_offset=tok_offset),
        grid_spec=pltpu.PrefetchScalarGridSpec(
            num_scalar_prefetch=1, grid=(b, t // tt),
            in_specs=[pl.BlockSpec(memory_space=pl.ANY), row(d), row(TOP_K), mod, vec, vec],
            out_specs=row(d),
            scratch_shapes=[pltpu.VMEM((2, TOP_K, tt, d), F32), pltpu.SemaphoreType.DMA((2,))]),
        out_shape=jax.ShapeDtypeStruct((b, t, d), F32),
        compiler_params=pltpu.CompilerParams(dimension_semantics=("arbitrary", "arbitrary"),
                                             vmem_limit_bytes=VMEM_LIMIT, disable_bounds_checks=True),
        name="moe_combine",
    )(pos, ys, x1, router_gates, gate, ln_g.reshape(1, d), ln_b.reshape(1, d))


def _route(logits):
    n_tok, n_exp = logits.shape
    top_logit, top_e = lax.top_k(logits, TOP_K)
    gates = jax.nn.softmax(top_logit, axis=-1)
    nk = n_tok * TOP_K
    flat_e = top_e.reshape(nk).astype(jnp.int32)
    onehot = (flat_e[:, None] == jnp.arange(n_exp, dtype=jnp.int32)[None, :]).astype(jnp.int32)
    running = jnp.cumsum(onehot, axis=0)
    counts = running[-1]
    rank = jnp.sum(running * onehot, axis=1) - 1
    padded = (counts + MOE_TM - 1) // MOE_TM * MOE_TM
    padded_end = jnp.cumsum(padded)
    padded_start = padded_end - padded
    start = jnp.cumsum(counts) - counts
    pos = (padded_start[flat_e] + rank).astype(jnp.int32)
    n_blocks = -(-nk // MOE_TM) + n_exp
    block_start = jnp.arange(n_blocks, dtype=jnp.int32) * MOE_TM
    block_expert = jnp.minimum(jnp.sum(block_start[:, None] >= padded_end[None, :], axis=1),
                               n_exp - 1).astype(jnp.int32)
    n_used = (padded_end[-1:] // MOE_TM).astype(jnp.int32)
    blk = jnp.arange(n_blocks, dtype=jnp.int32)
    run_start = (blk < n_used[0]) & ((blk == 0) | (block_expert != jnp.roll(block_expert, 1)))
    later_start = lax.cummin(jnp.where(run_start, blk, n_blocks)[::-1])[::-1]
    next_start = jnp.concatenate([later_start[1:], jnp.full((1,), n_blocks, jnp.int32)])
    next_expert = jnp.where(next_start < n_blocks,
                            block_expert[jnp.minimum(next_start, n_blocks - 1)], -1).astype(jnp.int32)
    order = jnp.argsort(flat_e).astype(jnp.int32)
    slot = jnp.arange(n_blocks * MOE_TM, dtype=jnp.int32)
    slot_e = jnp.repeat(block_expert, MOE_TM)
    within = slot - padded_start[slot_e]
    valid = within < counts[slot_e]
    src = order[jnp.clip(start[slot_e] + within, 0, nk - 1)]
    slot_tok = jnp.where(valid, src // TOP_K, 0).astype(jnp.int32)
    n_valid = jnp.clip(counts[block_expert] - (block_start - padded_start[block_expert]), 0, MOE_TM)
    n_valid = jnp.where(blk < n_used[0], n_valid, 0).astype(jnp.int32)
    sub_live = valid.reshape(-1, MOE_SUB)[:, 0].astype(jnp.int32)
    return slot_tok, gates, block_expert, n_used, next_expert, n_valid, sub_live, pos


def kernel(x_prompt, x_sample, c_prompt, c_sample, cache_sb_k, cache_sb_v, state_hgrn,
           w_ada, b_ada, w_in, w_out, norm_a, norm_b, lb_logits,
           ln1_g, ln1_b, ln2_g, ln2_b, router_w, router_b,
           w_gate, b_gate, w_up, b_up, w_down, b_down):
    depth = w_ada.shape[0]
    d = x_prompt.shape[-1]
    alpha = (2.0 * depth) ** 0.25
    d_sb = cache_sb_k.shape[2] * HEAD_DIM
    hg_heads = state_hgrn.shape[2]
    d_hk = hg_heads * state_hgrn.shape[3]
    d_hg = hg_heads * state_hgrn.shape[4]
    sb_scale = HEAD_DIM ** -0.5
    n_exp = router_w.shape[-1]
    lower_bounds = jnp.cumsum(jax.nn.softmax(lb_logits.astype(F32), axis=0), axis=0)
    streams = [x_prompt, x_sample]
    conds = [c_prompt, c_sample]
    n_rows = [c.shape[0] for c in conds]
    pad = (-sum(n_rows)) % 8
    c_all = jnp.concatenate(conds + [jnp.zeros((pad, d), F32)], axis=0)
    outs = [[] for _ in range(6)]

    for l in range(depth):
        mods = ada_modulation(c_all, w_ada[l], b_ada[l])
        w_in_l = w_in[l]
        seg = [0, d_sb, 2 * d_sb, 3 * d_sb, 3 * d_sb + d_hk, 3 * d_sb + 2 * d_hk,
               3 * d_sb + 2 * d_hk + d_hg, 3 * d_sb + 2 * d_hk + 2 * d_hg]
        w_seg = [w_in_l[:, seg[i]:seg[i + 1]].astype(BF16) for i in range(7)]
        wo_a = w_out[l, :d_sb].astype(BF16)
        wo_b = w_out[l, d_sb:].astype(BF16)
        gain_a = norm_a[l].reshape(-1, 1, HEAD_DIM)
        gain_b = norm_b[l].reshape(-1, 1, HEAD_DIM)
        lb = lower_bounds[l].reshape(hg_heads, 1, HEAD_DIM)
        x1s, h2s, lgs, gate2s = [], [], [], []
        row0 = 0
        for si, x in enumerate(streams):
            b = x.shape[0]
            m = mods[row0:row0 + b].reshape(b, N_MOD, 1, d)
            row0 += b
            shift1, scale1, gate1, shift2, scale2, gate2 = [m[:, i] for i in range(N_MOD)]
            h = modulate(x, scale1, shift1)
            (q_a,) = project_heads(h, w_seg[0], [BF16], scale=sb_scale)
            k_f, k_h = project_heads(h, w_seg[1], [F32, BF16])
            v_f, v_h = project_heads(h, w_seg[2], [F32, BF16])
            (q_b,) = project_heads(h, w_seg[3], [F32])
            (f_b,) = project_heads(h, w_seg[4], [F32])
            (i_b,) = project_heads(h, w_seg[5], [F32])
            (g_b,) = project_heads(h, w_seg[6], [F32])
            if si == 0:
                o_a = stick_breaking(q_a, k_h, v_h, gain_a)
                s0 = jnp.zeros((b, hg_heads, d_hk // hg_heads, HEAD_DIM), F32)
            else:
                o_a = stick_breaking(q_a, k_h, v_h, gain_a, cache_sb_k[l], cache_sb_v[l])
                s0 = state_hgrn[l]
            o_b, s_new = hgrn2(q_b, f_b, i_b, g_b, lb, s0, gain_b)
            x1, h2, lg = out_projection(o_a, o_b, wo_a, wo_b, x, gate1, scale2, shift2,
                                        ln1_g[l], ln1_b[l], router_w[l], router_b[l], alpha)
            x1s.append(x1)
            h2s.append(h2.reshape(-1, d))
            lgs.append(lg.reshape(-1, n_exp))
            gate2s.append(gate2)
            outs[3 * si + 0].append(k_f)
            outs[3 * si + 1].append(v_f)
            outs[3 * si + 2].append(s_new)

        (slot_tok, router_gates, block_expert, n_used, next_expert, n_valid, sub_live,
         pos) = _route(jnp.concatenate(lgs, axis=0))
        xs = gather_rows(jnp.concatenate(h2s, axis=0), slot_tok, sub_live)
        ys = expert_ffn(xs, block_expert, n_used, next_expert, n_valid, l,
                        w_gate, b_gate[l][:, None, :], w_up, b_up[l][:, None, :],
                        w_down, b_down[l][:, None, :])
        tok0 = 0
        new_streams = []
        for si, x1 in enumerate(x1s):
            n_s = x1.shape[0] * x1.shape[1]
            rg = router_gates[tok0:tok0 + n_s].reshape(x1.shape[0], x1.shape[1], TOP_K)
            new_streams.append(combine(ys, pos, rg, x1, gate2s[si], ln2_g[l], ln2_b[l], alpha, tok0))
            tok0 += n_s
        streams = new_streams

    stack = lambda xs_: jnp.stack(xs_)
    return (streams[0], streams[1], stack(outs[0]), stack(outs[1]), stack(outs[2]),
            stack(outs[3]), stack(outs[4]), stack(outs[5]))
```

```python
import functools
import math

import jax
import jax.numpy as jnp
from jax import lax
from jax.experimental import pallas as pl
from jax.experimental.pallas import tpu as pltpu

HEAD_DIM = 128
TOP_K = 4
SWIGLU_LIMIT = 7.0
SWIGLU_ALPHA = 1.702
LN_EPS = 1e-5
RMS_EPS = 1e-6
N_MOD = 6

LANES = 128
PROJ_ROWS = 1024
SB_TILE = 256
SB_HEADS_PROMPT = 2
SB_HEADS_CACHED = 4
SB_DEAD = -104.0
HG_GROUP = 16
HG_UNROLL = 8
HG_CHUNK = 64
HG_CHUNK_UNROLL = 4
HG_SAFE = 60.0
MOE_TM = 1024
MOE_SUB = 256
CAST_ROWS = 64
VMEM_LIMIT = 56 * 1024 * 1024

F32 = jnp.float32
BF16 = jnp.bfloat16


def _params(*sem):
    return pltpu.CompilerParams(dimension_semantics=sem, vmem_limit_bytes=VMEM_LIMIT)


def _dot(a, b):
    return jnp.dot(a, b, preferred_element_type=F32)


def _dot_nt(a, b):
    return lax.dot_general(a, b, (((1,), (1,)), ((), ())), preferred_element_type=F32)


def _dot_tn(a, b):
    return lax.dot_general(a, b, (((0,), (0,)), ((), ())), preferred_element_type=F32)


def _split3(x):
    hi = x.astype(BF16)
    r1 = x - hi.astype(F32)
    mid = r1.astype(BF16)
    lo = (r1 - mid.astype(F32)).astype(BF16)
    return hi, mid, lo


def _dot_exact_rhs(x, m):
    hi, mid, lo = _split3(x)
    return _dot(m, hi) + _dot(m, mid) + _dot(m, lo)


def _softplus(z):
    return jnp.maximum(z, 0.0) + jnp.log(1.0 + jnp.exp(-jnp.abs(z)))


def _ada_kernel(c_ref, w_ref, b_ref, o_ref):
    c = c_ref[...]
    s = (c * jax.nn.sigmoid(c)).astype(BF16)
    o_ref[...] = _dot(s, w_ref[...].astype(BF16)) + b_ref[...]


def ada_modulation(c, w, b):
    r, d = c.shape
    n = w.shape[1]
    tn = min(n, 512)
    return pl.pallas_call(
        _ada_kernel,
        grid=(n // tn,),
        in_specs=[pl.BlockSpec((r, d), lambda j: (0, 0)),
                  pl.BlockSpec((d, tn), lambda j: (0, j)),
                  pl.BlockSpec((1, tn), lambda j: (0, j))],
        out_specs=pl.BlockSpec((r, tn), lambda j: (0, j)),
        out_shape=jax.ShapeDtypeStruct((r, n), F32),
        compiler_params=_params("arbitrary"),
        name="ada_modulation",
    )(c, w, b.reshape(1, n))


def _modulate_kernel(x_ref, sc_ref, sh_ref, o_ref):
    o_ref[...] = (x_ref[...] * (1.0 + sc_ref[...]) + sh_ref[...]).astype(o_ref.dtype)


def _row_tiles(b, t, long_rows=512, short_rows=256):
    tt = min(t, long_rows)
    bb = max(1, min(b, short_rows // tt)) if tt < short_rows else 1
    return bb, tt


def modulate(x, scale, shift):
    b, t, d = x.shape
    bb, tt = _row_tiles(b, t)
    mod_spec = pl.BlockSpec((bb, 1, d), lambda i, j: (i, 0, 0))
    return pl.pallas_call(
        _modulate_kernel,
        grid=(b // bb, t // tt),
        in_specs=[pl.BlockSpec((bb, tt, d), lambda i, j: (i, j, 0)), mod_spec, mod_spec],
        out_specs=pl.BlockSpec((bb, tt, d), lambda i, j: (i, j, 0)),
        out_shape=jax.ShapeDtypeStruct((b, t, d), BF16),
        compiler_params=_params("parallel", "parallel"),
        name="modulate",
    )(x, scale, shift)


def _proj_kernel(h_ref, w_ref, *o_refs, scale):
    bb, tt, d = h_ref.shape
    acc = _dot(h_ref[...].reshape(bb * tt, d), w_ref[...])
    if scale != 1.0:
        acc = acc * scale
    heads = o_refs[0].shape[1]
    for hh in range(heads):
        blk = acc[:, hh * HEAD_DIM:(hh + 1) * HEAD_DIM].reshape(bb, tt, HEAD_DIM)
        for o_ref in o_refs:
            o_ref[:, hh] = blk.astype(o_ref.dtype)


def project_heads(h, w, out_dtypes, scale=1.0):
    b, t, d = h.shape
    n = w.shape[1]
    heads = n // HEAD_DIM
    bb, tt = _row_tiles(b, t, PROJ_ROWS, PROJ_ROWS)
    tn = min(n, 1024)
    hpb = tn // HEAD_DIM
    out_spec = pl.BlockSpec((bb, hpb, tt, HEAD_DIM), lambda i, j, k: (i, k, j, 0))
    outs = pl.pallas_call(
        functools.partial(_proj_kernel, scale=scale),
        grid=(b // bb, t // tt, n // tn),
        in_specs=[pl.BlockSpec((bb, tt, d), lambda i, j, k: (i, j, 0)),
                  pl.BlockSpec((d, tn), lambda i, j, k: (0, k))],
        out_specs=[out_spec] * len(out_dtypes),
        out_shape=[jax.ShapeDtypeStruct((b, heads, t, HEAD_DIM), dt) for dt in out_dtypes],
        compiler_params=_params("parallel", "parallel", "arbitrary"),
        name="project_heads",
    )(h, w)
    return outs


def _sb_block(q, k, v, u_ext, carry, row0):
    tk = k.shape[0]
    masked = row0 is not None
    z = _dot_nt(q, k)
    lf = -_softplus(z)
    if masked:
        row = lax.broadcasted_iota(jnp.int32, z.shape, 0) + row0
        col = lax.broadcasted_iota(jnp.int32, z.shape, 1)
        earlier = col < row
        lf = jnp.where(earlier, lf, 0.0)
    hi = lf.astype(BF16)
    lo = (lf - hi.astype(F32)).astype(BF16)
    cs = _dot(hi, u_ext) + _dot(lo, u_ext)
    if tk >= LANES:
        later = jnp.concatenate([carry] * (tk // LANES), axis=1)
    else:
        later = carry[:, :tk]
    after = cs[:, LANES:] + later
    w = jnp.exp(z + lf + after)
    if masked:
        w = jnp.where(earlier, w, 0.0)
    out = _dot(w.astype(BF16), v)
    return out, carry + cs[:, :LANES]


def _sb_kernel(q_ref, kd_ref, vd_ref, kp_ref, vp_ref, ud_ref, up_ref, gain_ref, o_ref,
               acc_ref, carry_ref, *, past_blocks):
    heads, tq = q_ref.shape[1], q_ref.shape[2]
    tk = up_ref.shape[0]
    n_past = pl.program_id(2) * (tq // tk) if past_blocks is None else past_blocks

    def own_keys(hd):
        return _sb_block(q_ref[0, hd], kd_ref[0, hd].astype(BF16), vd_ref[0, hd].astype(BF16),
                         ud_ref[...], jnp.zeros((tq, LANES), F32), row0=0)

    def past_block(hd, kb, carry):
        start = pl.multiple_of(kb * tk, tk)
        k = kp_ref[0, hd, pl.ds(start, tk), :].astype(BF16)
        v = vp_ref[0, hd, pl.ds(start, tk), :].astype(BF16)
        return _sb_block(q_ref[0, hd], k, v, up_ref[...], carry, row0=None)

    def own_and_nearest():
        for hd in range(heads):
            out_d, carry_d = own_keys(hd)
            out_p, carry_p = past_block(hd, n_past - 1, carry_d)
            acc_ref[hd] = out_d + out_p
            carry_ref[hd] = carry_p

    def own_only():
        for hd in range(heads):
            out_d, carry_d = own_keys(hd)
            acc_ref[hd] = out_d
            carry_ref[hd] = carry_d

    if past_blocks is None:
        pl.when(n_past > 0)(own_and_nearest)
        pl.when(n_past == 0)(own_only)
    elif past_blocks > 0:
        own_and_nearest()
    else:
        own_only()

    def cond(state):
        kb, alive = state
        return jnp.logical_and(kb >= 0, alive)

    def body(state):
        kb, _ = state
        for hd in range(heads):
            out, carry = past_block(hd, kb, carry_ref[hd])
            acc_ref[hd] += out
            carry_ref[hd] = carry
        return kb - 1, jnp.max(carry_ref[...]) > SB_DEAD

    lax.while_loop(cond, body, (n_past - 2, jnp.max(carry_ref[...]) > SB_DEAD))
    for hd in range(heads):
        o = acc_ref[hd]
        o = o * lax.rsqrt(jnp.mean(o * o, axis=-1, keepdims=True) + RMS_EPS)
        o_ref[0, :, hd * HEAD_DIM:(hd + 1) * HEAD_DIM] = (o * gain_ref[hd]).astype(o_ref.dtype)


def _suffix_matrix(tk):
    j = jnp.arange(tk)[:, None]
    c = jnp.arange(LANES + tk)[None, :]
    return jnp.where((c < LANES) | (j > c - LANES), 1.0, 0.0).astype(BF16)


def stick_breaking(q, k, v, gain, past_k=None, past_v=None):
    b, h, t, d = q.shape
    if past_k is None:
        tq = min(t, SB_TILE)
        tk = tq
        kp, vp, past_blocks = k, v, None
        hb = math.gcd(h, SB_HEADS_PROMPT)
    else:
        tq = t
        tk = min(past_k.shape[2], SB_TILE)
        kp, vp, past_blocks = past_k, past_v, past_k.shape[2] // tk
        hb = math.gcd(h, SB_HEADS_CACHED)
    p = kp.shape[2]
    tile_spec = pl.BlockSpec((1, hb, tq, d), lambda i, j, m: (i, j, m, 0))
    past_spec = pl.BlockSpec((1, hb, p, d), lambda i, j, m: (i, j, 0, 0))
    return pl.pallas_call(
        functools.partial(_sb_kernel, past_blocks=past_blocks),
        grid=(b, h // hb, t // tq),
        in_specs=[tile_spec, tile_spec, tile_spec, past_spec, past_spec,
                  pl.BlockSpec((tq, tq + LANES), lambda i, j, m: (0, 0)),
                  pl.BlockSpec((tk, tk + LANES), lambda i, j, m: (0, 0)),
                  pl.BlockSpec((hb, 1, d), lambda i, j, m: (j, 0, 0))],
        out_specs=pl.BlockSpec((1, tq, hb * d), lambda i, j, m: (i, m, j)),
        out_shape=jax.ShapeDtypeStruct((b, t, h * d), BF16),
        scratch_shapes=[pltpu.VMEM((hb, tq, LANES), F32), pltpu.VMEM((hb, tq, LANES), F32)],
        compiler_params=_params("parallel", "parallel", "arbitrary"),
        name="stick_breaking",
    )(q, k, v, kp, vp, _suffix_matrix(tq), _suffix_matrix(tk), gain)


def _blockwise_prefix(x, m):
    rows = m.shape[0]
    parts = [_dot_exact_rhs(x[r * rows:(r + 1) * rows], m) for r in range(x.shape[0] // rows)]
    return jnp.concatenate(parts, axis=0) if len(parts) > 1 else parts[0]


def _hgrn_kernel(q_ref, a_ref, i_ref, g_ref, lb_ref, s0_ref, gain_ref, cum_ref, tot_ref, ones_ref,
                 ccum_ref, ctot_ref, chalf_ref,
                 o_ref, s_ref, st_ref, qd_ref, kd_ref, vb_ref, dec_ref, od_ref, oi_ref):
    tt = q_ref.shape[2]
    ti = pl.program_id(2)

    @pl.when(ti == 0)
    def _():
        st_ref[...] = s0_ref[0, 0].T

    lb = lb_ref[0]
    a = a_ref[0, 0]
    log_sig = -_softplus(-a)
    x = jnp.log(lb)
    y = jnp.log1p(-lb) + log_sig
    log_f = jnp.maximum(x, y) + jnp.log(1.0 + jnp.exp(-jnp.abs(x - y)))
    kk = (1.0 - lb) * jax.nn.sigmoid(-a)
    q = q_ref[0, 0]
    v = i_ref[0, 0]

    vb_ref[...] = v.astype(BF16)

    b_half = _blockwise_prefix(log_f, chalf_ref[...])
    b_end = _blockwise_prefix(log_f, ctot_ref[...])
    safe = jnp.logical_and(jnp.min(b_half) > -HG_SAFE, jnp.min(b_end - b_half) > -HG_SAFE)

    def recurrence(step_rows, unroll):
        def step(i, st):
            r0 = pl.multiple_of(i * step_rows, step_rows)
            rows_i = pl.ds(r0, step_rows)
            oi_ref[rows_i, :] = _dot_nt(qd_ref[rows_i, :], st.astype(BF16))
            upd = _dot_tn(vb_ref[rows_i, :], kd_ref[rows_i, :])
            return st * dec_ref[pl.ds(r0, 1), :] + upd
        n = tt // step_rows
        st_ref[...] = lax.fori_loop(0, n, step, st_ref[...], unroll=min(n, unroll))

    @pl.when(safe)
    def _():
        ccum = ccum_ref[...]
        b_c = _blockwise_prefix(log_f, ccum)
        qd_ref[...] = (q * jnp.exp(b_c)).astype(BF16)
        kd_ref[...] = (kk * jnp.exp(b_end - b_c)).astype(BF16)
        dec_ref[...] = jnp.exp(b_end)
        qm = (q * jnp.exp(b_c - b_half)).astype(BF16)
        km = (kk * jnp.exp(b_half - b_c)).astype(BF16)
        same_chunk_earlier = ccum > 0
        rows = ccum.shape[0]
        for r in range(tt // rows):
            sl = slice(r * rows, (r + 1) * rows)
            sc = jnp.where(same_chunk_earlier, _dot_nt(qm[sl], km[sl]), 0.0)
            od_ref[sl, :] = _dot(sc.astype(BF16), vb_ref[sl, :])
        recurrence(min(tt, HG_CHUNK), HG_CHUNK_UNROLL)

    @pl.when(jnp.logical_not(safe))
    def _():
        ng = tt // HG_GROUP
        bcum = _blockwise_prefix(log_f, cum_ref[...])
        bend = _blockwise_prefix(log_f, tot_ref[...])
        qd_ref[...] = (q * jnp.exp(bcum)).astype(BF16)
        kd_ref[...] = (kk * jnp.exp(bend - bcum)).astype(BF16)
        dec_ref[...] = jnp.exp(bend)
        b3 = bcum.reshape(ng, HG_GROUP, HEAD_DIM)
        q3 = q.reshape(ng, HG_GROUP, HEAD_DIM)
        k3 = kk.reshape(ng, HG_GROUP, HEAD_DIM)
        v3 = v.reshape(ng, HG_GROUP, HEAD_DIM)
        pos = lax.broadcasted_iota(jnp.int32, (ng, HG_GROUP, HEAD_DIM), 1)
        ones = ones_ref[...]
        od = jnp.zeros((tt, HEAD_DIM), F32)
        for s in range(HG_GROUP):
            diff = b3 - b3[:, s:s + 1, :]
            wgt = jnp.where(pos >= s, jnp.exp(jnp.minimum(diff, 0.0)), 0.0)
            prod = (q3 * k3[:, s:s + 1, :] * wgt).reshape(tt, HEAD_DIM)
            score = _dot(prod.astype(BF16), ones)
            vs = jnp.broadcast_to(v3[:, s:s + 1, :], (ng, HG_GROUP, HEAD_DIM)).reshape(tt, HEAD_DIM)
            od = od + score * vs
        od_ref[...] = od
        recurrence(HG_GROUP, HG_UNROLL)

    o = od_ref[...] + oi_ref[...]
    o = o * lax.rsqrt(jnp.mean(o * o, axis=-1, keepdims=True) + RMS_EPS) * gain_ref[0]
    gg = g_ref[0, 0]
    o_ref[0] = (o * (gg * jax.nn.sigmoid(gg))).astype(o_ref.dtype)

    @pl.when(ti == pl.num_programs(2) - 1)
    def _():
        s_ref[0, 0] = st_ref[...].T


def _span_matrices(rows, span):
    r = jnp.arange(rows)
    same = (r[:, None] // span) == (r[None, :] // span)
    cum = jnp.where(same & (r[None, :] <= r[:, None]), 1.0, 0.0).astype(BF16)
    tot = jnp.where(same, 1.0, 0.0).astype(BF16)
    half = jnp.where(same & ((r[None, :] % span) < span // 2), 1.0, 0.0).astype(BF16)
    return cum, tot, half


def hgrn2(q, a, i, g, lb, s0, gain):
    b, h, t, d = q.shape
    tt = min(t, 1024)
    rows = min(tt, LANES)
    cum, tot, _ = _span_matrices(rows, HG_GROUP)
    ccum, ctot, chalf = _span_matrices(rows, min(tt, HG_CHUNK))
    ones = jnp.ones((d, d), BF16)
    seq_spec = pl.BlockSpec((1, 1, tt, d), lambda bi, hi, ti: (bi, hi, ti, 0))
    head_spec = pl.BlockSpec((1, 1, d), lambda bi, hi, ti: (hi, 0, 0))
    state_spec = pl.BlockSpec((1, 1, d, d), lambda bi, hi, ti: (bi, hi, 0, 0))
    const = lambda shape: pl.BlockSpec(shape, lambda bi, hi, ti: (0, 0))
    return pl.pallas_call(
        _hgrn_kernel,
        grid=(b, h, t // tt),
        in_specs=[seq_spec, seq_spec, seq_spec, seq_spec, head_spec, state_spec, head_spec,
                  const((rows, rows)), const((rows, rows)), const((d, d)),
                  const((rows, rows)), const((rows, rows)), const((rows, rows))],
        out_specs=[pl.BlockSpec((1, tt, d), lambda bi, hi, ti: (bi, ti, hi)), state_spec],
        out_shape=[jax.ShapeDtypeStruct((b, t, h * d), BF16),
                   jax.ShapeDtypeStruct((b, h, d, d), F32)],
        scratch_shapes=([pltpu.VMEM((d, d), F32)] + [pltpu.VMEM((tt, d), BF16)] * 3
                        + [pltpu.VMEM((tt, d), F32)] * 3),
        compiler_params=_params("parallel", "parallel", "arbitrary"),
        name="hgrn2",
    )(q, a, i, g, lb, s0, gain, cum, tot, ones, ccum, ctot, chalf)


def _layer_norm(x, g, b):
    mu = jnp.mean(x, axis=-1, keepdims=True)
    xc = x - mu
    var = jnp.mean(xc * xc, axis=-1, keepdims=True)
    return xc * lax.rsqrt(var + LN_EPS) * g + b


def _mix_kernel(oa_ref, ob_ref, wa_ref, wb_ref, o_ref):
    bb, tt, dh = oa_ref.shape
    part = (_dot(oa_ref[...].reshape(bb * tt, dh), wa_ref[...])
            + _dot(ob_ref[...].reshape(bb * tt, dh), wb_ref[...]))
    o_ref[...] = part.reshape(bb, tt, part.shape[-1])


def _post_mix_kernel(mix_ref, x_ref, gate_ref, sc_ref, sh_ref, lng_ref, lnb_ref, rw_ref, rb_ref,
                     x1_ref, h2_ref, lg_ref, *, alpha):
    bb, tt, d = x_ref.shape
    x1 = _layer_norm(alpha * x_ref[...] + gate_ref[...] * mix_ref[...], lng_ref[...], lnb_ref[...])
    x1_ref[...] = x1
    h2 = x1 * (1.0 + sc_ref[...]) + sh_ref[...]
    h2_ref[...] = h2.astype(h2_ref.dtype)
    hh, hm, hl = _split3(h2.reshape(bb * tt, d))
    w = rw_ref[...]
    wh = w.astype(BF16)
    wl = (w - wh.astype(F32)).astype(BF16)
    lg = (_dot(hh, wh) + (_dot(hm, wh) + _dot(hh, wl))) + (_dot(hl, wh) + _dot(hm, wl))
    lg_ref[...] = (lg + rb_ref[...]).reshape(bb, tt, lg.shape[-1])


def out_projection(oa, ob, wa, wb, x, gate, scale, shift, ln_g, ln_b, router_w, router_b, alpha):
    b, t, dh = oa.shape
    d = x.shape[2]
    e = router_w.shape[1]
    bb, tt = _row_tiles(b, t, PROJ_ROWS, PROJ_ROWS)
    tn = min(d, 1024)
    mix = pl.pallas_call(
        _mix_kernel,
        grid=(b // bb, t // tt, d // tn),
        in_specs=[pl.BlockSpec((bb, tt, dh), lambda i, j, n: (i, j, 0)),
                  pl.BlockSpec((bb, tt, dh), lambda i, j, n: (i, j, 0)),
                  pl.BlockSpec((dh, tn), lambda i, j, n: (0, n)),
                  pl.BlockSpec((dh, tn), lambda i, j, n: (0, n))],
        out_specs=pl.BlockSpec((bb, tt, tn), lambda i, j, n: (i, j, n)),
        out_shape=jax.ShapeDtypeStruct((b, t, d), F32),
        compiler_params=_params("parallel", "parallel", "arbitrary"),
        name="out_projection",
    )(oa, ob, wa, wb)
    bb, tt = _row_tiles(b, t, 256, 256)
    row = lambda w_: pl.BlockSpec((bb, tt, w_), lambda i, j: (i, j, 0))
    mod = pl.BlockSpec((bb, 1, d), lambda i, j: (i, 0, 0))
    vec = lambda w_: pl.BlockSpec((1, w_), lambda i, j: (0, 0))
    return pl.pallas_call(
        functools.partial(_post_mix_kernel, alpha=alpha),
        grid=(b // bb, t // tt),
        in_specs=[row(d), row(d), mod, mod, mod, vec(d), vec(d),
                  pl.BlockSpec((d, e), lambda i, j: (0, 0)), vec(e)],
        out_specs=[row(d), row(d), row(e)],
        out_shape=[jax.ShapeDtypeStruct((b, t, d), F32), jax.ShapeDtypeStruct((b, t, d), F32),
                   jax.ShapeDtypeStruct((b, t, e), F32)],
        compiler_params=_params("parallel", "parallel"),
        name="post_mix",
    )(mix, x, gate, scale, shift, ln_g.reshape(1, d), ln_b.reshape(1, d),
      router_w, router_b.reshape(1, e))


DMA_UNROLL = 8


def _row_copy_loops(copy, n):
    def issue(blk, slot):
        def body(g, c):
            r0 = pl.multiple_of(g * DMA_UNROLL, DMA_UNROLL)
            for u in range(DMA_UNROLL):
                copy(blk, r0, u, slot).start()
            return c
        lax.fori_loop(0, n // DMA_UNROLL, body, 0)

    def drain(slot):
        def body(g, c):
            for _ in range(DMA_UNROLL):
                copy(0, 0, 0, slot).wait()
            return c
        lax.fori_loop(0, n // DMA_UNROLL, body, 0)

    return issue, drain


def _gather_kernel(tok_ref, live_ref, src_ref, o_ref, buf_ref, sem):
    i = pl.program_id(0)
    slot = i % 2

    def copy(blk, r0, u, sl):
        return pltpu.make_async_copy(src_ref.at[pl.ds(tok_ref[blk * MOE_SUB + r0 + u], 1), :],
                                     buf_ref.at[sl, pl.ds(r0 + u, 1), :], sem.at[sl])

    issue, drain = _row_copy_loops(copy, MOE_SUB)

    @pl.when(jnp.logical_and(i == 0, live_ref[0] > 0))
    def _():
        issue(0, 0)

    nxt = jnp.minimum(i + 1, pl.num_programs(0) - 1)

    @pl.when(jnp.logical_and(i + 1 < pl.num_programs(0), live_ref[nxt] > 0))
    def _():
        issue(i + 1, 1 - slot)

    @pl.when(live_ref[i] > 0)
    def _():
        drain(slot)
        o_ref[...] = buf_ref[slot].astype(o_ref.dtype)

    @pl.when(live_ref[i] == 0)
    def _():
        o_ref[...] = jnp.zeros_like(o_ref)


def gather_rows(src, tok, sub_live):
    r = tok.shape[0]
    d = src.shape[1]
    return pl.pallas_call(
        _gather_kernel,
        grid_spec=pltpu.PrefetchScalarGridSpec(
            num_scalar_prefetch=2, grid=(r // MOE_SUB,),
            in_specs=[pl.BlockSpec(memory_space=pl.ANY)],
            out_specs=pl.BlockSpec((MOE_SUB, d), lambda i, tok_, lv_: (i, 0)),
            scratch_shapes=[pltpu.VMEM((2, MOE_SUB, d), F32), pltpu.SemaphoreType.DMA((2,))]),
        out_shape=jax.ShapeDtypeStruct((r, d), BF16),
        compiler_params=pltpu.CompilerParams(dimension_semantics=("arbitrary",),
                                             vmem_limit_bytes=VMEM_LIMIT, disable_bounds_checks=True),
        name="gather_rows",
    )(tok, sub_live, src)


def _expert_changed(be_ref, i):
    return jnp.logical_or(i == 0, be_ref[i] != be_ref[jnp.maximum(i - 1, 0)])


def _stream_expert_weights(be_ref, nb_ref, nx_ref, w_hbms, wf_ref, wb_refs, sem, layer):
    j, i = pl.program_id(0), pl.program_id(1)
    tn = wf_ref.shape[2]
    live = i < nb_ref[0]

    def copies(e):
        cols = pl.ds(pl.multiple_of(j * tn, tn), tn)
        return [pltpu.make_async_copy(w.at[layer, e, :, cols], wf_ref.at[m], sem.at[m])
                for m, w in enumerate(w_hbms)]

    @pl.when(jnp.logical_and(live, i == 0))
    def _():
        for c in copies(be_ref[0]):
            c.start()

    @pl.when(jnp.logical_and(live, _expert_changed(be_ref, i)))
    def _():
        for c in copies(be_ref[i]):
            c.wait()
        def convert(c, carry):
            rows = pl.ds(pl.multiple_of(c * CAST_ROWS, CAST_ROWS), CAST_ROWS)
            for m, wb_ref in enumerate(wb_refs):
                wb_ref[rows, :] = wf_ref[m, rows, :].astype(BF16)
            return carry

        lax.fori_loop(0, wf_ref.shape[1] // CAST_ROWS, convert, 0)
        nxt = nx_ref[i]

        @pl.when(nxt >= 0)
        def _():
            for c in copies(nxt):
                c.start()

    return live


def _for_filled_rows(live, n_valid, o_ref, compute):
    tm = o_ref.shape[0]
    for rows in range(MOE_SUB, tm + 1, MOE_SUB):
        @pl.when(jnp.logical_and(live, jnp.logical_and(n_valid > rows - MOE_SUB, n_valid <= rows)))
        def _(rows=rows):
            o_ref[:rows, :] = compute(rows).astype(o_ref.dtype)
            if rows < tm:
                o_ref[rows:, :] = jnp.zeros((tm - rows, o_ref.shape[1]), o_ref.dtype)

    @pl.when(jnp.logical_not(live))
    def _():
        o_ref[...] = jnp.zeros_like(o_ref)


def _gate_up_kernel(be_ref, nb_ref, nx_ref, nv_ref, x_ref, wg_hbm, bg_ref, wu_hbm, bu_ref, o_ref,
                    wf_ref, wgb_ref, wub_ref, sem, *, layer):
    live = _stream_expert_weights(be_ref, nb_ref, nx_ref, (wg_hbm, wu_hbm), wf_ref,
                                  (wgb_ref, wub_ref), sem, layer)

    def compute(rows):
        x = x_ref[:rows, :]
        gate = jnp.minimum(_dot(x, wgb_ref[...]) + bg_ref[0], SWIGLU_LIMIT)
        up = jnp.clip(_dot(x, wub_ref[...]) + bu_ref[0], -SWIGLU_LIMIT, SWIGLU_LIMIT)
        return (up + 1.0) * gate * jax.nn.sigmoid(SWIGLU_ALPHA * gate)

    _for_filled_rows(live, nv_ref[pl.program_id(1)], o_ref, compute)


def _down_kernel(be_ref, nb_ref, nx_ref, nv_ref, a_ref, w_hbm, b_ref, o_ref, wf_ref, wb_ref, sem,
                 *, layer):
    live = _stream_expert_weights(be_ref, nb_ref, nx_ref, (w_hbm,), wf_ref, (wb_ref,), sem, layer)

    def compute(rows):
        return _dot(a_ref[:rows, :], wb_ref[...]) + b_ref[0]

    _for_filled_rows(live, nv_ref[pl.program_id(1)], o_ref, compute)


def expert_ffn(xs, block_expert, n_used, next_expert, n_valid, layer,
               w_gate, b_gate, w_up, b_up, w_down, b_down):
    r, d = xs.shape
    f = w_gate.shape[3]
    nb = r // MOE_TM
    hbm = pl.BlockSpec(memory_space=pl.ANY)
    bspec = lambda n_: pl.BlockSpec((1, 1, n_), lambda j, i, be, nu, nx, nv: (be[i], 0, j))
    rows = lambda w_: pl.BlockSpec((MOE_TM, w_), lambda j, i, be, nu, nx, nv: (i, 0))
    tile = lambda n_: pl.BlockSpec((MOE_TM, n_), lambda j, i, be, nu, nx, nv: (i, j))
    tn = min(f, 512)
    act = pl.pallas_call(
        functools.partial(_gate_up_kernel, layer=layer),
        grid_spec=pltpu.PrefetchScalarGridSpec(
            num_scalar_prefetch=4, grid=(f // tn, nb),
            in_specs=[rows(d), hbm, bspec(tn), hbm, bspec(tn)],
            out_specs=tile(tn),
            scratch_shapes=[pltpu.VMEM((2, d, tn), F32), pltpu.VMEM((d, tn), BF16),
                            pltpu.VMEM((d, tn), BF16), pltpu.SemaphoreType.DMA((2,))]),
        out_shape=jax.ShapeDtypeStruct((r, f), BF16),
        compiler_params=_params("arbitrary", "arbitrary"),
        name="expert_gate_up",
    )(block_expert, n_used, next_expert, n_valid, xs, w_gate, b_gate, w_up, b_up)
    tn = min(d, 1024)
    return pl.pallas_call(
        functools.partial(_down_kernel, layer=layer),
        grid_spec=pltpu.PrefetchScalarGridSpec(
            num_scalar_prefetch=4, grid=(d // tn, nb),
            in_specs=[rows(f), hbm, bspec(tn)],
            out_specs=tile(tn),
            scratch_shapes=[pltpu.VMEM((1, f, tn), F32), pltpu.VMEM((f, tn), BF16),
                            pltpu.SemaphoreType.DMA((1,))]),
        out_shape=jax.ShapeDtypeStruct((r, d), F32),
        compiler_params=_params("arbitrary", "arbitrary"),
        name="expert_down",
    )(block_expert, n_used, next_expert, n_valid, act, w_down, b_down)


COMBINE_TOKENS = 64


def _combine_kernel(pos_ref, ys_ref, x1_ref, rg_ref, gate_ref, lng_ref, lnb_ref, o_ref, buf_ref, sem,
                    *, alpha, tok_offset):
    _, tt, d = x1_ref.shape
    step = pl.program_id(0) * pl.num_programs(1) + pl.program_id(1)
    n_steps = pl.num_programs(0) * pl.num_programs(1)
    slot = step % 2

    def copy(stp, r0, u, sl):
        t = r0 // TOP_K + u // TOP_K
        src_row = pos_ref[(tok_offset + stp * tt) * TOP_K + r0 + u]
        return pltpu.make_async_copy(ys_ref.at[pl.ds(src_row, 1), :],
                                     buf_ref.at[sl, u % TOP_K, pl.ds(t, 1), :], sem.at[sl])

    issue, drain = _row_copy_loops(copy, tt * TOP_K)

    @pl.when(step == 0)
    def _():
        issue(0, 0)

    @pl.when(step + 1 < n_steps)
    def _():
        issue(step + 1, 1 - slot)

    drain(slot)
    rg = rg_ref[0]
    ffn = ((rg[:, 0:1] * buf_ref[slot, 0] + rg[:, 1:2] * buf_ref[slot, 1])
           + (rg[:, 2:3] * buf_ref[slot, 2] + rg[:, 3:4] * buf_ref[slot, 3]))
    y = alpha * x1_ref[0] + gate_ref[0] * ffn
    o_ref[0] = _layer_norm(y, lng_ref[...], lnb_ref[...])


def combine(ys, pos, router_gates, x1, gate, ln_g, ln_b, alpha, tok_offset):
    b, t, d = x1.shape
    tt = min(t, COMBINE_TOKENS)
    row = lambda w_: pl.BlockSpec((1, tt, w_), lambda i, j, p: (i, j, 0))
    mod = pl.BlockSpec((1, 1, d), lambda i, j, p: (i, 0, 0))
    vec = pl.BlockSpec((1, d), lambda i, j, p: (0, 0))
    return pl.pallas_call(
        functools.partial(_combine_kernel, alpha=alpha, tok_offset=tok_offset),
        grid_spec=pltpu.PrefetchScalarGridSpec(
            num_scalar_prefetch=1, grid=(b, t // tt),
            in_specs=[pl.BlockSpec(memory_space=pl.ANY), row(d), row(TOP_K), mod, vec, vec],
            out_specs=row(d),
            scratch_shapes=[pltpu.VMEM((2, TOP_K, tt, d), F32), pltpu.SemaphoreType.DMA((2,))]),
        out_shape=jax.ShapeDtypeStruct((b, t, d), F32),
        compiler_params=pltpu.CompilerParams(dimension_semantics=("arbitrary", "arbitrary"),
                                             vmem_limit_bytes=VMEM_LIMIT, disable_bounds_checks=True),
        name="moe_combine",
    )(pos, ys, x1, router_gates, gate, ln_g.reshape(1, d), ln_b.reshape(1, d))


def _route(logits):
    n_tok, n_exp = logits.shape
    top_logit, top_e = lax.top_k(logits, TOP_K)
    gates = jax.nn.softmax(top_logit, axis=-1)
    nk = n_tok * TOP_K
    flat_e = top_e.reshape(nk).astype(jnp.int32)
    onehot = (flat_e[:, None] == jnp.arange(n_exp, dtype=jnp.int32)[None, :]).astype(jnp.int32)
    running = jnp.cumsum(onehot, axis=0)
    counts = running[-1]
    rank = jnp.sum(running * onehot, axis=1) - 1
    padded = (counts + MOE_TM - 1) // MOE_TM * MOE_TM
    padded_end = jnp.cumsum(padded)
    padded_start = padded_end - padded
    start = jnp.cumsum(counts) - counts
    pos = (padded_start[flat_e] + rank).astype(jnp.int32)
    n_blocks = -(-nk // MOE_TM) + n_exp
    block_start = jnp.arange(n_blocks, dtype=jnp.int32) * MOE_TM
    block_expert = jnp.minimum(jnp.sum(block_start[:, None] >= padded_end[None, :], axis=1),
                               n_exp - 1).astype(jnp.int32)
    n_used = (padded_end[-1:] // MOE_TM).astype(jnp.int32)
    blk = jnp.arange(n_blocks, dtype=jnp.int32)
    run_start = (blk < n_used[0]) & ((blk == 0) | (block_expert != jnp.roll(block_expert, 1)))
    later_start = lax.cummin(jnp.where(run_start, blk, n_blocks)[::-1])[::-1]
    next_start = jnp.concatenate([later_start[1:], jnp.full((1,), n_blocks, jnp.int32)])
    next_expert = jnp.where(next_start < n_blocks,
                            block_expert[jnp.minimum(next_start, n_blocks - 1)], -1).astype(jnp.int32)
    order = jnp.argsort(flat_e).astype(jnp.int32)
    slot = jnp.arange(n_blocks * MOE_TM, dtype=jnp.int32)
    slot_e = jnp.repeat(block_expert, MOE_TM)
    within = slot - padded_start[slot_e]
    valid = within < counts[slot_e]
    src = order[jnp.clip(start[slot_e] + within, 0, nk - 1)]
    slot_tok = jnp.where(valid, src // TOP_K, 0).astype(jnp.int32)
    n_valid = jnp.clip(counts[block_expert] - (block_start - padded_start[block_expert]), 0, MOE_TM)
    n_valid = jnp.where(blk < n_used[0], n_valid, 0).astype(jnp.int32)
    sub_live = valid.reshape(-1, MOE_SUB)[:, 0].astype(jnp.int32)
    return slot_tok, gates, block_expert, n_used, next_expert, n_valid, sub_live, pos


def kernel(x_prompt, x_sample, c_prompt, c_sample, cache_sb_k, cache_sb_v, state_hgrn,
           w_ada, b_ada, w_in, w_out, norm_a, norm_b, lb_logits,
           ln1_g, ln1_b, ln2_g, ln2_b, router_w, router_b,
           w_gate, b_gate, w_up, b_up, w_down, b_down):
    depth = w_ada.shape[0]
    d = x_prompt.shape[-1]
    alpha = (2.0 * depth) ** 0.25
    d_sb = cache_sb_k.shape[2] * HEAD_DIM
    hg_heads = state_hgrn.shape[2]
    d_hk = hg_heads * state_hgrn.shape[3]
    d_hg = hg_heads * state_hgrn.shape[4]
    sb_scale = HEAD_DIM ** -0.5
    n_exp = router_w.shape[-1]
    lower_bounds = jnp.cumsum(jax.nn.softmax(lb_logits.astype(F32), axis=0), axis=0)
    streams = [x_prompt, x_sample]
    conds = [c_prompt, c_sample]
    n_rows = [c.shape[0] for c in conds]
    pad = (-sum(n_rows)) % 8
    c_all = jnp.concatenate(conds + [jnp.zeros((pad, d), F32)], axis=0)
    outs = [[] for _ in range(6)]

    for l in range(depth):
        mods = ada_modulation(c_all, w_ada[l], b_ada[l])
        w_in_l = w_in[l]
        seg = [0, d_sb, 2 * d_sb, 3 * d_sb, 3 * d_sb + d_hk, 3 * d_sb + 2 * d_hk,
               3 * d_sb + 2 * d_hk + d_hg, 3 * d_sb + 2 * d_hk + 2 * d_hg]
        w_seg = [w_in_l[:, seg[i]:seg[i + 1]].astype(BF16) for i in range(7)]
        wo_a = w_out[l, :d_sb].astype(BF16)
        wo_b = w_out[l, d_sb:].astype(BF16)
        gain_a = norm_a[l].reshape(-1, 1, HEAD_DIM)
        gain_b = norm_b[l].reshape(-1, 1, HEAD_DIM)
        lb = lower_bounds[l].reshape(hg_heads, 1, HEAD_DIM)
        x1s, h2s, lgs, gate2s = [], [], [], []
        row0 = 0
        for si, x in enumerate(streams):
            b = x.shape[0]
            m = mods[row0:row0 + b].reshape(b, N_MOD, 1, d)
            row0 += b
            shift1, scale1, gate1, shift2, scale2, gate2 = [m[:, i] for i in range(N_MOD)]
            h = modulate(x, scale1, shift1)
            (q_a,) = project_heads(h, w_seg[0], [BF16], scale=sb_scale)
            k_f, k_h = project_heads(h, w_seg[1], [F32, BF16])
            v_f, v_h = project_heads(h, w_seg[2], [F32, BF16])
            (q_b,) = project_heads(h, w_seg[3], [F32])
            (f_b,) = project_heads(h, w_seg[4], [F32])
            (i_b,) = project_heads(h, w_seg[5], [F32])
            (g_b,) = project_heads(h, w_seg[6], [F32])
            if si == 0:
                o_a = stick_breaking(q_a, k_h, v_h, gain_a)
                s0 = jnp.zeros((b, hg_heads, d_hk // hg_heads, HEAD_DIM), F32)
            else:
                o_a = stick_breaking(q_a, k_h, v_h, gain_a, cache_sb_k[l], cache_sb_v[l])
                s0 = state_hgrn[l]
            o_b, s_new = hgrn2(q_b, f_b, i_b, g_b, lb, s0, gain_b)
            x1, h2, lg = out_projection(o_a, o_b, wo_a, wo_b, x, gate1, scale2, shift2,
                                        ln1_g[l], ln1_b[l], router_w[l], router_b[l], alpha)
            x1s.append(x1)
            h2s.append(h2.reshape(-1, d))
            lgs.append(lg.reshape(-1, n_exp))
            gate2s.append(gate2)
            outs[3 * si + 0].append(k_f)
            outs[3 * si + 1].append(v_f)
            outs[3 * si + 2].append(s_new)

        (slot_tok, router_gates, block_expert, n_used, next_expert, n_valid, sub_live,
         pos) = _route(jnp.concatenate(lgs, axis=0))
        xs = gather_rows(jnp.concatenate(h2s, axis=0), slot_tok, sub_live)
        ys = expert_ffn(xs, block_expert, n_used, next_expert, n_valid, l,
                        w_gate, b_gate[l][:, None, :], w_up, b_up[l][:, None, :],
                        w_down, b_down[l][:, None, :])
        tok0 = 0
        new_streams = []
        for si, x1 in enumerate(x1s):
            n_s = x1.shape[0] * x1.shape[1]
            rg = router_gates[tok0:tok0 + n_s].reshape(x1.shape[0], x1.shape[1], TOP_K)
            new_streams.append(combine(ys, pos, rg, x1, gate2s[si], ln2_g[l], ln2_b[l], alpha, tok0))
            tok0 += n_s
        streams = new_streams

    stack = lambda xs_: jnp.stack(xs_)
    return (streams[0], streams[1], stack(outs[0]), stack(outs[1]), stack(outs[2]),
            stack(outs[3]), stack(outs[4]), stack(outs[5]))
```

```python
import functools
import math

import jax
import jax.numpy as jnp
from jax import lax
from jax.experimental import pallas as pl
from jax.experimental.pallas import tpu as pltpu

HEAD_DIM = 128
TOP_K = 4
SWIGLU_LIMIT = 7.0
SWIGLU_ALPHA = 1.702
LN_EPS = 1e-5
RMS_EPS = 1e-6
N_MOD = 6

LANES = 128
PROJ_ROWS = 1024
SB_TILE = 256
SB_HEADS_PROMPT = 2
SB_HEADS_CACHED = 4
SB_DEAD = -104.0
HG_GROUP = 16
HG_UNROLL = 8
HG_CHUNK = 64
HG_CHUNK_UNROLL = 4
HG_SAFE = 60.0
MOE_TM = 1024
MOE_SUB = 256
CAST_ROWS = 64
VMEM_LIMIT = 56 * 1024 * 1024

F32 = jnp.float32
BF16 = jnp.bfloat16


def _params(*sem):
    return pltpu.CompilerParams(dimension_semantics=sem, vmem_limit_bytes=VMEM_LIMIT)


def _dot(a, b):
    return jnp.dot(a, b, preferred_element_type=F32)


def _dot_nt(a, b):
    return lax.dot_general(a, b, (((1,), (1,)), ((), ())), preferred_element_type=F32)


def _dot_tn(a, b):
    return lax.dot_general(a, b, (((0,), (0,)), ((), ())), preferred_element_type=F32)


def _split3(x):
    hi = x.astype(BF16)
    r1 = x - hi.astype(F32)
    mid = r1.astype(BF16)
    lo = (r1 - mid.astype(F32)).astype(BF16)
    return hi, mid, lo


def _dot_exact_rhs(x, m):
    hi, mid, lo = _split3(x)
    return _dot(m, hi) + _dot(m, mid) + _dot(m, lo)


def _softplus(z):
    return jnp.maximum(z, 0.0) + jnp.log(1.0 + jnp.exp(-jnp.abs(z)))


def _ada_kernel(c_ref, w_ref, b_ref, o_ref):
    c = c_ref[...]
    s = (c * jax.nn.sigmoid(c)).astype(BF16)
    o_ref[...] = _dot(s, w_ref[...].astype(BF16)) + b_ref[...]


def ada_modulation(c, w, b):
    r, d = c.shape
    n = w.shape[1]
    tn = min(n, 512)
    return pl.pallas_call(
        _ada_kernel,
        grid=(n // tn,),
        in_specs=[pl.BlockSpec((r, d), lambda j: (0, 0)),
                  pl.BlockSpec((d, tn), lambda j: (0, j)),
                  pl.BlockSpec((1, tn), lambda j: (0, j))],
        out_specs=pl.BlockSpec((r, tn), lambda j: (0, j)),
        out_shape=jax.ShapeDtypeStruct((r, n), F32),
        compiler_params=_params("arbitrary"),
        name="ada_modulation",
    )(c, w, b.reshape(1, n))


def _modulate_kernel(x_ref, sc_ref, sh_ref, o_ref):
    o_ref[...] = (x_ref[...] * (1.0 + sc_ref[...]) + sh_ref[...]).astype(o_ref.dtype)


def _row_tiles(b, t, long_rows=512, short_rows=256):
    tt = min(t, long_rows)
    bb = max(1, min(b, short_rows // tt)) if tt < short_rows else 1
    return bb, tt


def modulate(x, scale, shift):
    b, t, d = x.shape
    bb, tt = _row_tiles(b, t)
    mod_spec = pl.BlockSpec((bb, 1, d), lambda i, j: (i, 0, 0))
    return pl.pallas_call(
        _modulate_kernel,
        grid=(b // bb, t // tt),
        in_specs=[pl.BlockSpec((bb, tt, d), lambda i, j: (i, j, 0)), mod_spec, mod_spec],
        out_specs=pl.BlockSpec((bb, tt, d), lambda i, j: (i, j, 0)),
        out_shape=jax.ShapeDtypeStruct((b, t, d), BF16),
        compiler_params=_params("parallel", "parallel"),
        name="modulate",
    )(x, scale, shift)


def _proj_kernel(h_ref, w_ref, *o_refs, scale):
    bb, tt, d = h_ref.shape
    acc = _dot(h_ref[...].reshape(bb * tt, d), w_ref[...])
    if scale != 1.0:
        acc = acc * scale
    heads = o_refs[0].shape[1]
    for hh in range(heads):
        blk = acc[:, hh * HEAD_DIM:(hh + 1) * HEAD_DIM].reshape(bb, tt, HEAD_DIM)
        for o_ref in o_refs:
            o_ref[:, hh] = blk.astype(o_ref.dtype)


def project_heads(h, w, out_dtypes, scale=1.0):
    b, t, d = h.shape
    n = w.shape[1]
    heads = n // HEAD_DIM
    bb, tt = _row_tiles(b, t, PROJ_ROWS, PROJ_ROWS)
    tn = min(n, 1024)
    hpb = tn // HEAD_DIM
    out_spec = pl.BlockSpec((bb, hpb, tt, HEAD_DIM), lambda i, j, k: (i, k, j, 0))
    outs = pl.pallas_call(
        functools.partial(_proj_kernel, scale=scale),
        grid=(b // bb, t // tt, n // tn),
        in_specs=[pl.BlockSpec((bb, tt, d), lambda i, j, k: (i, j, 0)),
                  pl.BlockSpec((d, tn), lambda i, j, k: (0, k))],
        out_specs=[out_spec] * len(out_dtypes),
        out_shape=[jax.ShapeDtypeStruct((b, heads, t, HEAD_DIM), dt) for dt in out_dtypes],
        compiler_params=_params("parallel", "parallel", "arbitrary"),
        name="project_heads",
    )(h, w)
    return outs


def _sb_block(q, k, v, u_ext, carry, row0):
    tk = k.shape[0]
    masked = row0 is not None
    z = _dot_nt(q, k)
    lf = -_softplus(z)
    if masked:
        row = lax.broadcasted_iota(jnp.int32, z.shape, 0) + row0
        col = lax.broadcasted_iota(jnp.int32, z.shape, 1)
        earlier = col < row
        lf = jnp.where(earlier, lf, 0.0)
    hi = lf.astype(BF16)
    lo = (lf - hi.astype(F32)).astype(BF16)
    cs = _dot(hi, u_ext) + _dot(lo, u_ext)
    if tk >= LANES:
        later = jnp.concatenate([carry] * (tk // LANES), axis=1)
    else:
        later = carry[:, :tk]
    after = cs[:, LANES:] + later
    w = jnp.exp(z + lf + after)
    if masked:
        w = jnp.where(earlier, w, 0.0)
    out = _dot(w.astype(BF16), v)
    return out, carry + cs[:, :LANES]


def _sb_kernel(q_ref, kd_ref, vd_ref, kp_ref, vp_ref, ud_ref, up_ref, gain_ref, o_ref,
               acc_ref, carry_ref, *, past_blocks):
    heads, tq = q_ref.shape[1], q_ref.shape[2]
    tk = up_ref.shape[0]
    n_past = pl.program_id(2) * (tq // tk) if past_blocks is None else past_blocks

    def own_keys(hd):
        return _sb_block(q_ref[0, hd], kd_ref[0, hd].astype(BF16), vd_ref[0, hd].astype(BF16),
                         ud_ref[...], jnp.zeros((tq, LANES), F32), row0=0)

    def past_block(hd, kb, carry):
        start = pl.multiple_of(kb * tk, tk)
        k = kp_ref[0, hd, pl.ds(start, tk), :].astype(BF16)
        v = vp_ref[0, hd, pl.ds(start, tk), :].astype(BF16)
        return _sb_block(q_ref[0, hd], k, v, up_ref[...], carry, row0=None)

    def own_and_nearest():
        for hd in range(heads):
            out_d, carry_d = own_keys(hd)
            out_p, carry_p = past_block(hd, n_past - 1, carry_d)
            acc_ref[hd] = out_d + out_p
            carry_ref[hd] = carry_p

    def own_only():
        for hd in range(heads):
            out_d, carry_d = own_keys(hd)
            acc_ref[hd] = out_d
            carry_ref[hd] = carry_d

    if past_blocks is None:
        pl.when(n_past > 0)(own_and_nearest)
        pl.when(n_past == 0)(own_only)
    elif past_blocks > 0:
        own_and_nearest()
    else:
        own_only()

    def cond(state):
        kb, alive = state
        return jnp.logical_and(kb >= 0, alive)

    def body(state):
        kb, _ = state
        for hd in range(heads):
            out, carry = past_block(hd, kb, carry_ref[hd])
            acc_ref[hd] += out
            carry_ref[hd] = carry
        return kb - 1, jnp.max(carry_ref[...]) > SB_DEAD

    lax.while_loop(cond, body, (n_past - 2, jnp.max(carry_ref[...]) > SB_DEAD))
    for hd in range(heads):
        o = acc_ref[hd]
        o = o * lax.rsqrt(jnp.mean(o * o, axis=-1, keepdims=True) + RMS_EPS)
        o_ref[0, :, hd * HEAD_DIM:(hd + 1) * HEAD_DIM] = (o * gain_ref[hd]).astype(o_ref.dtype)


def _suffix_matrix(tk):
    j = jnp.arange(tk)[:, None]
    c = jnp.arange(LANES + tk)[None, :]
    return jnp.where((c < LANES) | (j > c - LANES), 1.0, 0.0).astype(BF16)


def stick_breaking(q, k, v, gain, past_k=None, past_v=None):
    b, h, t, d = q.shape
    if past_k is None:
        tq = min(t, SB_TILE)
        tk = tq
        kp, vp, past_blocks = k, v, None
        hb = math.gcd(h, SB_HEADS_PROMPT)
    else:
        tq = t
        tk = min(past_k.shape[2], SB_TILE)
        kp, vp, past_blocks = past_k, past_v, past_k.shape[2] // tk
        hb = math.gcd(h, SB_HEADS_CACHED)
    p = kp.shape[2]
    tile_spec = pl.BlockSpec((1, hb, tq, d), lambda i, j, m: (i, j, m, 0))
    past_spec = pl.BlockSpec((1, hb, p, d), lambda i, j, m: (i, j, 0, 0))
    return pl.pallas_call(
        functools.partial(_sb_kernel, past_blocks=past_blocks),
        grid=(b, h // hb, t // tq),
        in_specs=[tile_spec, tile_spec, tile_spec, past_spec, past_spec,
                  pl.BlockSpec((tq, tq + LANES), lambda i, j, m: (0, 0)),
                  pl.BlockSpec((tk, tk + LANES), lambda i, j, m: (0, 0)),
                  pl.BlockSpec((hb, 1, d), lambda i, j, m: (j, 0, 0))],
        out_specs=pl.BlockSpec((1, tq, hb * d), lambda i, j, m: (i, m, j)),
        out_shape=jax.ShapeDtypeStruct((b, t, h * d), BF16),
        scratch_shapes=[pltpu.VMEM((hb, tq, LANES), F32), pltpu.VMEM((hb, tq, LANES), F32)],
        compiler_params=_params("parallel", "parallel", "arbitrary"),
        name="stick_breaking",
    )(q, k, v, kp, vp, _suffix_matrix(tq), _suffix_matrix(tk), gain)


def _blockwise_prefix(x, m):
    rows = m.shape[0]
    parts = [_dot_exact_rhs(x[r * rows:(r + 1) * rows], m) for r in range(x.shape[0] // rows)]
    return jnp.concatenate(parts, axis=0) if len(parts) > 1 else parts[0]


def _hgrn_kernel(q_ref, a_ref, i_ref, g_ref, lb_ref, s0_ref, gain_ref, cum_ref, tot_ref, ones_ref,
                 ccum_ref, ctot_ref, chalf_ref,
                 o_ref, s_ref, st_ref, qd_ref, kd_ref, vb_ref, dec_ref, od_ref, oi_ref):
    tt = q_ref.shape[2]
    ti = pl.program_id(2)

    @pl.when(ti == 0)
    def _():
        st_ref[...] = s0_ref[0, 0].T

    lb = lb_ref[0]
    a = a_ref[0, 0]
    log_sig = -_softplus(-a)
    x = jnp.log(lb)
    y = jnp.log1p(-lb) + log_sig
    log_f = jnp.maximum(x, y) + jnp.log(1.0 + jnp.exp(-jnp.abs(x - y)))
    kk = (1.0 - lb) * jax.nn.sigmoid(-a)
    q = q_ref[0, 0]
    v = i_ref[0, 0]

    vb_ref[...] = v.astype(BF16)

    b_half = _blockwise_prefix(log_f, chalf_ref[...])
    b_end = _blockwise_prefix(log_f, ctot_ref[...])
    safe = jnp.logical_and(jnp.min(b_half) > -HG_SAFE, jnp.min(b_end - b_half) > -HG_SAFE)

    def recurrence(step_rows, unroll):
        def step(i, st):
            r0 = pl.multiple_of(i * step_rows, step_rows)
            rows_i = pl.ds(r0, step_rows)
            oi_ref[rows_i, :] = _dot_nt(qd_ref[rows_i, :], st.astype(BF16))
            upd = _dot_tn(vb_ref[rows_i, :], kd_ref[rows_i, :])
            return st * dec_ref[pl.ds(r0, 1), :] + upd
        n = tt // step_rows
        st_ref[...] = lax.fori_loop(0, n, step, st_ref[...], unroll=min(n, unroll))

    @pl.when(safe)
    def _():
        ccum = ccum_ref[...]
        b_c = _blockwise_prefix(log_f, ccum)
        qd_ref[...] = (q * jnp.exp(b_c)).astype(BF16)
        kd_ref[...] = (kk * jnp.exp(b_end - b_c)).astype(BF16)
        dec_ref[...] = jnp.exp(b_end)
        qm = (q * jnp.exp(b_c - b_half)).astype(BF16)
        km = (kk * jnp.exp(b_half - b_c)).astype(BF16)
        same_chunk_earlier = ccum > 0
        rows = ccum.shape[0]
        for r in range(tt // rows):
            sl = slice(r * rows, (r + 1) * rows)
            sc = jnp.where(same_chunk_earlier, _dot_nt(qm[sl], km[sl]), 0.0)
            od_ref[sl, :] = _dot(sc.astype(BF16), vb_ref[sl, :])
        recurrence(min(tt, HG_CHUNK), HG_CHUNK_UNROLL)

    @pl.when(jnp.logical_not(safe))
    def _():
        ng = tt // HG_GROUP
        bcum = _blockwise_prefix(log_f, cum_ref[...])
        bend = _blockwise_prefix(log_f, tot_ref[...])
        qd_ref[...] = (q * jnp.exp(bcum)).astype(BF16)
        kd_ref[...] = (kk * jnp.exp(bend - bcum)).astype(BF16)
        dec_ref[...] = jnp.exp(bend)
        b3 = bcum.reshape(ng, HG_GROUP, HEAD_DIM)
        q3 = q.reshape(ng, HG_GROUP, HEAD_DIM)
        k3 = kk.reshape(ng, HG_GROUP, HEAD_DIM)
        v3 = v.reshape(ng, HG_GROUP, HEAD_DIM)
        pos = lax.broadcasted_iota(jnp.int32, (ng, HG_GROUP, HEAD_DIM), 1)
        ones = ones_ref[...]
        od = jnp.zeros((tt, HEAD_DIM), F32)
        for s in range(HG_GROUP):
            diff = b3 - b3[:, s:s + 1, :]
            wgt = jnp.where(pos >= s, jnp.exp(jnp.minimum(diff, 0.0)), 0.0)
            prod = (q3 * k3[:, s:s + 1, :] * wgt).reshape(tt, HEAD_DIM)
            score = _dot(prod.astype(BF16), ones)
            vs = jnp.broadcast_to(v3[:, s:s + 1, :], (ng, HG_GROUP, HEAD_DIM)).reshape(tt, HEAD_DIM)
            od = od + score * vs
        od_ref[...] = od
        recurrence(HG_GROUP, HG_UNROLL)

    o = od_ref[...] + oi_ref[...]
    o = o * lax.rsqrt(jnp.mean(o * o, axis=-1, keepdims=True) + RMS_EPS) * gain_ref[0]
    gg = g_ref[0, 0]
    o_ref[0] = (o * (gg * jax.nn.sigmoid(gg))).astype(o_ref.dtype)

    @pl.when(ti == pl.num_programs(2) - 1)
    def _():
        s_ref[0, 0] = st_ref[...].T


def _span_matrices(rows, span):
    r = jnp.arange(rows)
    same = (r[:, None] // span) == (r[None, :] // span)
    cum = jnp.where(same & (r[None, :] <= r[:, None]), 1.0, 0.0).astype(BF16)
    tot = jnp.where(same, 1.0, 0.0).astype(BF16)
    half = jnp.where(same & ((r[None, :] % span) < span // 2), 1.0, 0.0).astype(BF16)
    return cum, tot, half


def hgrn2(q, a, i, g, lb, s0, gain):
    b, h, t, d = q.shape
    tt = min(t, 1024)
    rows = min(tt, LANES)
    cum, tot, _ = _span_matrices(rows, HG_GROUP)
    ccum, ctot, chalf = _span_matrices(rows, min(tt, HG_CHUNK))
    ones = jnp.ones((d, d), BF16)
    seq_spec = pl.BlockSpec((1, 1, tt, d), lambda bi, hi, ti: (bi, hi, ti, 0))
    head_spec = pl.BlockSpec((1, 1, d), lambda bi, hi, ti: (hi, 0, 0))
    state_spec = pl.BlockSpec((1, 1, d, d), lambda bi, hi, ti: (bi, hi, 0, 0))
    const = lambda shape: pl.BlockSpec(shape, lambda bi, hi, ti: (0, 0))
    return pl.pallas_call(
        _hgrn_kernel,
        grid=(b, h, t // tt),
        in_specs=[seq_spec, seq_spec, seq_spec, seq_spec, head_spec, state_spec, head_spec,
                  const((rows, rows)), const((rows, rows)), const((d, d)),
                  const((rows, rows)), const((rows, rows)), const((rows, rows))],
        out_specs=[pl.BlockSpec((1, tt, d), lambda bi, hi, ti: (bi, ti, hi)), state_spec],
        out_shape=[jax.ShapeDtypeStruct((b, t, h * d), BF16),
                   jax.ShapeDtypeStruct((b, h, d, d), F32)],
        scratch_shapes=([pltpu.VMEM((d, d), F32)] + [pltpu.VMEM((tt, d), BF16)] * 3
                        + [pltpu.VMEM((tt, d), F32)] * 3),
        compiler_params=_params("parallel", "parallel", "arbitrary"),
        name="hgrn2",
    )(q, a, i, g, lb, s0, gain, cum, tot, ones, ccum, ctot, chalf)


def _layer_norm(x, g, b):
    mu = jnp.mean(x, axis=-1, keepdims=True)
    xc = x - mu
    var = jnp.mean(xc * xc, axis=-1, keepdims=True)
    return xc * lax.rsqrt(var + LN_EPS) * g + b


def _mix_kernel(oa_ref, ob_ref, wa_ref, wb_ref, o_ref):
    bb, tt, dh = oa_ref.shape
    part = (_dot(oa_ref[...].reshape(bb * tt, dh), wa_ref[...])
            + _dot(ob_ref[...].reshape(bb * tt, dh), wb_ref[...]))
    o_ref[...] = part.reshape(bb, tt, part.shape[-1])


def _post_mix_kernel(mix_ref, x_ref, gate_ref, sc_ref, sh_ref, lng_ref, lnb_ref, rw_ref, rb_ref,
                     x1_ref, h2_ref, lg_ref, *, alpha):
    bb, tt, d = x_ref.shape
    x1 = _layer_norm(alpha * x_ref[...] + gate_ref[...] * mix_ref[...], lng_ref[...], lnb_ref[...])
    x1_ref[...] = x1
    h2 = x1 * (1.0 + sc_ref[...]) + sh_ref[...]
    h2_ref[...] = h2.astype(h2_ref.dtype)
    hh, hm, hl = _split3(h2.reshape(bb * tt, d))
    w = rw_ref[...]
    wh = w.astype(BF16)
    wl = (w - wh.astype(F32)).astype(BF16)
    lg = (_dot(hh, wh) + (_dot(hm, wh) + _dot(hh, wl))) + (_dot(hl, wh) + _dot(hm, wl))
    lg_ref[...] = (lg + rb_ref[...]).reshape(bb, tt, lg.shape[-1])


def out_projection(oa, ob, wa, wb, x, gate, scale, shift, ln_g, ln_b, router_w, router_b, alpha):
    b, t, dh = oa.shape
    d = x.shape[2]
    e = router_w.shape[1]
    bb, tt = _row_tiles(b, t, PROJ_ROWS, PROJ_ROWS)
    tn = min(d, 1024)
    mix = pl.pallas_call(
        _mix_kernel,
        grid=(b // bb, t // tt, d // tn),
        in_specs=[pl.BlockSpec((bb, tt, dh), lambda i, j, n: (i, j, 0)),
                  pl.BlockSpec((bb, tt, dh), lambda i, j, n: (i, j, 0)),
                  pl.BlockSpec((dh, tn), lambda i, j, n: (0, n)),
                  pl.BlockSpec((dh, tn), lambda i, j, n: (0, n))],
        out_specs=pl.BlockSpec((bb, tt, tn), lambda i, j, n: (i, j, n)),
        out_shape=jax.ShapeDtypeStruct((b, t, d), F32),
        compiler_params=_params("parallel", "parallel", "arbitrary"),
        name="out_projection",
    )(oa, ob, wa, wb)
    bb, tt = _row_tiles(b, t, 256, 256)
    row = lambda w_: pl.BlockSpec((bb, tt, w_), lambda i, j: (i, j, 0))
    mod = pl.BlockSpec((bb, 1, d), lambda i, j: (i, 0, 0))
    vec = lambda w_: pl.BlockSpec((1, w_), lambda i, j: (0, 0))
    return pl.pallas_call(
        functools.partial(_post_mix_kernel, alpha=alpha),
        grid=(b // bb, t // tt),
        in_specs=[row(d), row(d), mod, mod, mod, vec(d), vec(d),
                  pl.BlockSpec((d, e), lambda i, j: (0, 0)), vec(e)],
        out_specs=[row(d), row(d), row(e)],
        out_shape=[jax.ShapeDtypeStruct((b, t, d), F32), jax.ShapeDtypeStruct((b, t, d), F32),
                   jax.ShapeDtypeStruct((b, t, e), F32)],
        compiler_params=_params("parallel", "parallel"),
        name="post_mix",
    )(mix, x, gate, scale, shift, ln_g.reshape(1, d), ln_b.reshape(1, d),
      router_w, router_b.reshape(1, e))


DMA_UNROLL = 8


def _row_copy_loops(copy, n):
    def issue(blk, slot):
        def body(g, c):
            r0 = pl.multiple_of(g * DMA_UNROLL, DMA_UNROLL)
            for u in range(DMA_UNROLL):
                copy(blk, r0, u, slot).start(priority=u % 2)
            return c
        lax.fori_loop(0, n // DMA_UNROLL, body, 0)

    def drain(slot):
        def body(g, c):
            for _ in range(DMA_UNROLL):
                copy(0, 0, 0, slot).wait()
            return c
        lax.fori_loop(0, n // DMA_UNROLL, body, 0)

    return issue, drain


def _gather_kernel(tok_ref, live_ref, src_ref, o_ref, buf_ref, sem):
    i = pl.program_id(0)
    slot = i % 2

    def copy(blk, r0, u, sl):
        return pltpu.make_async_copy(src_ref.at[pl.ds(tok_ref[blk * MOE_SUB + r0 + u], 1), :],
                                     buf_ref.at[sl, pl.ds(r0 + u, 1), :], sem.at[sl])

    issue, drain = _row_copy_loops(copy, MOE_SUB)

    @pl.when(jnp.logical_and(i == 0, live_ref[0] > 0))
    def _():
        issue(0, 0)

    nxt = jnp.minimum(i + 1, pl.num_programs(0) - 1)

    @pl.when(jnp.logical_and(i + 1 < pl.num_programs(0), live_ref[nxt] > 0))
    def _():
        issue(i + 1, 1 - slot)

    @pl.when(live_ref[i] > 0)
    def _():
        drain(slot)
        o_ref[...] = buf_ref[slot].astype(o_ref.dtype)

    @pl.when(live_ref[i] == 0)
    def _():
        o_ref[...] = jnp.zeros_like(o_ref)


def gather_rows(src, tok, sub_live):
    r = tok.shape[0]
    d = src.shape[1]
    return pl.pallas_call(
        _gather_kernel,
        grid_spec=pltpu.PrefetchScalarGridSpec(
            num_scalar_prefetch=2, grid=(r // MOE_SUB,),
            in_specs=[pl.BlockSpec(memory_space=pl.ANY)],
            out_specs=pl.BlockSpec((MOE_SUB, d), lambda i, tok_, lv_: (i, 0)),
            scratch_shapes=[pltpu.VMEM((2, MOE_SUB, d), F32), pltpu.SemaphoreType.DMA((2,))]),
        out_shape=jax.ShapeDtypeStruct((r, d), BF16),
        compiler_params=pltpu.CompilerParams(dimension_semantics=("arbitrary",),
                                             vmem_limit_bytes=VMEM_LIMIT, disable_bounds_checks=True),
        name="gather_rows",
    )(tok, sub_live, src)


def _expert_changed(be_ref, i):
    return jnp.logical_or(i == 0, be_ref[i] != be_ref[jnp.maximum(i - 1, 0)])


def _stream_expert_weights(be_ref, nb_ref, nx_ref, w_hbms, wf_ref, wb_refs, sem, layer):
    j, i = pl.program_id(0), pl.program_id(1)
    tn = wf_ref.shape[2]
    live = i < nb_ref[0]

    def copies(e):
        cols = pl.ds(pl.multiple_of(j * tn, tn), tn)
        return [pltpu.make_async_copy(w.at[layer, e, :, cols], wf_ref.at[m], sem.at[m])
                for m, w in enumerate(w_hbms)]

    @pl.when(jnp.logical_and(live, i == 0))
    def _():
        for c in copies(be_ref[0]):
            c.start()

    @pl.when(jnp.logical_and(live, _expert_changed(be_ref, i)))
    def _():
        for c in copies(be_ref[i]):
            c.wait()
        def convert(c, carry):
            rows = pl.ds(pl.multiple_of(c * CAST_ROWS, CAST_ROWS), CAST_ROWS)
            for m, wb_ref in enumerate(wb_refs):
                wb_ref[rows, :] = wf_ref[m, rows, :].astype(BF16)
            return carry

        lax.fori_loop(0, wf_ref.shape[1] // CAST_ROWS, convert, 0)
        nxt = nx_ref[i]

        @pl.when(nxt >= 0)
        def _():
            for c in copies(nxt):
                c.start()

    return live


def _for_filled_rows(live, n_valid, o_ref, compute):
    tm = o_ref.shape[0]
    for rows in range(MOE_SUB, tm + 1, MOE_SUB):
        @pl.when(jnp.logical_and(live, jnp.logical_and(n_valid > rows - MOE_SUB, n_valid <= rows)))
        def _(rows=rows):
            o_ref[:rows, :] = compute(rows).astype(o_ref.dtype)
            if rows < tm:
                o_ref[rows:, :] = jnp.zeros((tm - rows, o_ref.shape[1]), o_ref.dtype)

    @pl.when(jnp.logical_not(live))
    def _():
        o_ref[...] = jnp.zeros_like(o_ref)


def _gate_up_kernel(be_ref, nb_ref, nx_ref, nv_ref, x_ref, wg_hbm, bg_ref, wu_hbm, bu_ref, o_ref,
                    wf_ref, wgb_ref, wub_ref, sem, *, layer):
    live = _stream_expert_weights(be_ref, nb_ref, nx_ref, (wg_hbm, wu_hbm), wf_ref,
                                  (wgb_ref, wub_ref), sem, layer)

    def compute(rows):
        x = x_ref[:rows, :]
        gate = jnp.minimum(_dot(x, wgb_ref[...]) + bg_ref[0], SWIGLU_LIMIT)
        up = jnp.clip(_dot(x, wub_ref[...]) + bu_ref[0], -SWIGLU_LIMIT, SWIGLU_LIMIT)
        return (up + 1.0) * gate * jax.nn.sigmoid(SWIGLU_ALPHA * gate)

    _for_filled_rows(live, nv_ref[pl.program_id(1)], o_ref, compute)


def _down_kernel(be_ref, nb_ref, nx_ref, nv_ref, a_ref, w_hbm, b_ref, o_ref, wf_ref, wb_ref, sem,
                 *, layer):
    live = _stream_expert_weights(be_ref, nb_ref, nx_ref, (w_hbm,), wf_ref, (wb_ref,), sem, layer)

    def compute(rows):
        return _dot(a_ref[:rows, :], wb_ref[...]) + b_ref[0]

    _for_filled_rows(live, nv_ref[pl.program_id(1)], o_ref, compute)


def expert_ffn(xs, block_expert, n_used, next_expert, n_valid, layer,
               w_gate, b_gate, w_up, b_up, w_down, b_down):
    r, d = xs.shape
    f = w_gate.shape[3]
    nb = r // MOE_TM
    hbm = pl.BlockSpec(memory_space=pl.ANY)
    bspec = lambda n_: pl.BlockSpec((1, 1, n_), lambda j, i, be, nu, nx, nv: (be[i], 0, j))
    rows = lambda w_: pl.BlockSpec((MOE_TM, w_), lambda j, i, be, nu, nx, nv: (i, 0))
    tile = lambda n_: pl.BlockSpec((MOE_TM, n_), lambda j, i, be, nu, nx, nv: (i, j))
    tn = min(f, 512)
    act = pl.pallas_call(
        functools.partial(_gate_up_kernel, layer=layer),
        grid_spec=pltpu.PrefetchScalarGridSpec(
            num_scalar_prefetch=4, grid=(f // tn, nb),
            in_specs=[rows(d), hbm, bspec(tn), hbm, bspec(tn)],
            out_specs=tile(tn),
            scratch_shapes=[pltpu.VMEM((2, d, tn), F32), pltpu.VMEM((d, tn), BF16),
                            pltpu.VMEM((d, tn), BF16), pltpu.SemaphoreType.DMA((2,))]),
        out_shape=jax.ShapeDtypeStruct((r, f), BF16),
        compiler_params=_params("arbitrary", "arbitrary"),
        name="expert_gate_up",
    )(block_expert, n_used, next_expert, n_valid, xs, w_gate, b_gate, w_up, b_up)
    tn = min(d, 1024)
    return pl.pallas_call(
        functools.partial(_down_kernel, layer=layer),
        grid_spec=pltpu.PrefetchScalarGridSpec(
            num_scalar_prefetch=4, grid=(d // tn, nb),
            in_specs=[rows(f), hbm, bspec(tn)],
            out_specs=tile(tn),
            scratch_shapes=[pltpu.VMEM((1, f, tn), F32), pltpu.VMEM((f, tn), BF16),
                            pltpu.SemaphoreType.DMA((1,))]),
        out_shape=jax.ShapeDtypeStruct((r, d), F32),
        compiler_params=_params("arbitrary", "arbitrary"),
        name="expert_down",
    )(block_expert, n_used, next_expert, n_valid, act, w_down, b_down)


COMBINE_TOKENS = 64


def _combine_kernel(pos_ref, ys_ref, x1_ref, rg_ref, gate_ref, lng_ref, lnb_ref, o_ref, buf_ref, sem,
                    *, alpha, tok_offset):
    _, tt, d = x1_ref.shape
    step = pl.program_id(0) * pl.num_programs(1) + pl.program_id(1)
    n_steps = pl.num_programs(0) * pl.num_programs(1)
    slot = step % 2

    def copy(stp, r0, u, sl):
        t = r0 // TOP_K + u // TOP_K
        src_row = pos_ref[(tok_offset + stp * tt) * TOP_K + r0 + u]
        return pltpu.make_async_copy(ys_ref.at[pl.ds(src_row, 1), :],
                                     buf_ref.at[sl, u % TOP_K, pl.ds(t, 1), :], sem.at[sl])

    issue, drain = _row_copy_loops(copy, tt * TOP_K)

    @pl.when(step == 0)
    def _():
        issue(0, 0)

    @pl.when(step + 1 < n_steps)
    def _():
        issue(step + 1, 1 - slot)

    drain(slot)
    rg = rg_ref[0]
    ffn = ((rg[:, 0:1] * buf_ref[slot, 0] + rg[:, 1:2] * buf_ref[slot, 1])
           + (rg[:, 2:3] * buf_ref[slot, 2] + rg[:, 3:4] * buf_ref[slot, 3]))
    y = alpha * x1_ref[0] + gate_ref[0] * ffn
    o_ref[0] = _layer_norm(y, lng_ref[...], lnb_ref[...])


def combine(ys, pos, router_gates, x1, gate, ln_g, ln_b, alpha, tok_offset):
    b, t, d = x1.shape
    tt = min(t, COMBINE_TOKENS)
    row = lambda w_: pl.BlockSpec((1, tt, w_), lambda i, j, p: (i, j, 0))
    mod = pl.BlockSpec((1, 1, d), lambda i, j, p: (i, 0, 0))
    vec = pl.BlockSpec((1, d), lambda i, j, p: (0, 0))
    return pl.pallas_call(
        functools.partial(_combine_kernel, alpha=alpha, tok_offset=tok_offset),
        grid_spec=pltpu.PrefetchScalarGridSpec(
            num_scalar_prefetch=1, grid=(b, t // tt),
            in_specs=[pl.BlockSpec(memory_space=pl.ANY), row(d), row(TOP_K), mod, vec, vec],
            out_specs=row(d),
            scratch_shapes=[pltpu.VMEM((2, TOP_K, tt, d), F32), pltpu.SemaphoreType.DMA((2,))]),
        out_shape=jax.ShapeDtypeStruct((b, t, d), F32),
        compiler_params=pltpu.CompilerParams(dimension_semantics=("arbitrary", "arbitrary"),
                                             vmem_limit_bytes=VMEM_LIMIT, disable_bounds_checks=True),
        name="moe_combine",
    )(pos, ys, x1, router_gates, gate, ln_g.reshape(1, d), ln_b.reshape(1, d))


def _route(logits):
    n_tok, n_exp = logits.shape
    top_logit, top_e = lax.top_k(logits, TOP_K)
    gates = jax.nn.softmax(top_logit, axis=-1)
    nk = n_tok * TOP_K
    flat_e = top_e.reshape(nk).astype(jnp.int32)
    onehot = (flat_e[:, None] == jnp.arange(n_exp, dtype=jnp.int32)[None, :]).astype(jnp.int32)
    running = jnp.cumsum(onehot, axis=0)
    counts = running[-1]
    rank = jnp.sum(running * onehot, axis=1) - 1
    padded = (counts + MOE_TM - 1) // MOE_TM * MOE_TM
    padded_end = jnp.cumsum(padded)
    padded_start = padded_end - padded
    start = jnp.cumsum(counts) - counts
    pos = (padded_start[flat_e] + rank).astype(jnp.int32)
    n_blocks = -(-nk // MOE_TM) + n_exp
    block_start = jnp.arange(n_blocks, dtype=jnp.int32) * MOE_TM
    block_expert = jnp.minimum(jnp.sum(block_start[:, None] >= padded_end[None, :], axis=1),
                               n_exp - 1).astype(jnp.int32)
    n_used = (padded_end[-1:] // MOE_TM).astype(jnp.int32)
    blk = jnp.arange(n_blocks, dtype=jnp.int32)
    run_start = (blk < n_used[0]) & ((blk == 0) | (block_expert != jnp.roll(block_expert, 1)))
    later_start = lax.cummin(jnp.where(run_start, blk, n_blocks)[::-1])[::-1]
    next_start = jnp.concatenate([later_start[1:], jnp.full((1,), n_blocks, jnp.int32)])
    next_expert = jnp.where(next_start < n_blocks,
                            block_expert[jnp.minimum(next_start, n_blocks - 1)], -1).astype(jnp.int32)
    order = jnp.argsort(flat_e).astype(jnp.int32)
    slot = jnp.arange(n_blocks * MOE_TM, dtype=jnp.int32)
    slot_e = jnp.repeat(block_expert, MOE_TM)
    within = slot - padded_start[slot_e]
    valid = within < counts[slot_e]
    src = order[jnp.clip(start[slot_e] + within, 0, nk - 1)]
    slot_tok = jnp.where(valid, src // TOP_K, 0).astype(jnp.int32)
    n_valid = jnp.clip(counts[block_expert] - (block_start - padded_start[block_expert]), 0, MOE_TM)
    n_valid = jnp.where(blk < n_used[0], n_valid, 0).astype(jnp.int32)
    sub_live = valid.reshape(-1, MOE_SUB)[:, 0].astype(jnp.int32)
    return slot_tok, gates, block_expert, n_used, next_expert, n_valid, sub_live, pos


def kernel(x_prompt, x_sample, c_prompt, c_sample, cache_sb_k, cache_sb_v, state_hgrn,
           w_ada, b_ada, w_in, w_out, norm_a, norm_b, lb_logits,
           ln1_g, ln1_b, ln2_g, ln2_b, router_w, router_b,
           w_gate, b_gate, w_up, b_up, w_down, b_down):
    depth = w_ada.shape[0]
    d = x_prompt.shape[-1]
    alpha = (2.0 * depth) ** 0.25
    d_sb = cache_sb_k.shape[2] * HEAD_DIM
    hg_heads = state_hgrn.shape[2]
    d_hk = hg_heads * state_hgrn.shape[3]
    d_hg = hg_heads * state_hgrn.shape[4]
    sb_scale = HEAD_DIM ** -0.5
    n_exp = router_w.shape[-1]
    lower_bounds = jnp.cumsum(jax.nn.softmax(lb_logits.astype(F32), axis=0), axis=0)
    streams = [x_prompt, x_sample]
    conds = [c_prompt, c_sample]
    n_rows = [c.shape[0] for c in conds]
    pad = (-sum(n_rows)) % 8
    c_all = jnp.concatenate(conds + [jnp.zeros((pad, d), F32)], axis=0)
    outs = [[] for _ in range(6)]

    for l in range(depth):
        mods = ada_modulation(c_all, w_ada[l], b_ada[l])
        w_in_l = w_in[l]
        seg = [0, d_sb, 2 * d_sb, 3 * d_sb, 3 * d_sb + d_hk, 3 * d_sb + 2 * d_hk,
               3 * d_sb + 2 * d_hk + d_hg, 3 * d_sb + 2 * d_hk + 2 * d_hg]
        w_seg = [w_in_l[:, seg[i]:seg[i + 1]].astype(BF16) for i in range(7)]
        wo_a = w_out[l, :d_sb].astype(BF16)
        wo_b = w_out[l, d_sb:].astype(BF16)
        gain_a = norm_a[l].reshape(-1, 1, HEAD_DIM)
        gain_b = norm_b[l].reshape(-1, 1, HEAD_DIM)
        lb = lower_bounds[l].reshape(hg_heads, 1, HEAD_DIM)
        x1s, h2s, lgs, gate2s = [], [], [], []
        row0 = 0
        for si, x in enumerate(streams):
            b = x.shape[0]
            m = mods[row0:row0 + b].reshape(b, N_MOD, 1, d)
            row0 += b
            shift1, scale1, gate1, shift2, scale2, gate2 = [m[:, i] for i in range(N_MOD)]
            h = modulate(x, scale1, shift1)
            (q_a,) = project_heads(h, w_seg[0], [BF16], scale=sb_scale)
            k_f, k_h = project_heads(h, w_seg[1], [F32, BF16])
            v_f, v_h = project_heads(h, w_seg[2], [F32, BF16])
            (q_b,) = project_heads(h, w_seg[3], [F32])
            (f_b,) = project_heads(h, w_seg[4], [F32])
            (i_b,) = project_heads(h, w_seg[5], [F32])
            (g_b,) = project_heads(h, w_seg[6], [F32])
            if si == 0:
                o_a = stick_breaking(q_a, k_h, v_h, gain_a)
                s0 = jnp.zeros((b, hg_heads, d_hk // hg_heads, HEAD_DIM), F32)
            else:
                o_a = stick_breaking(q_a, k_h, v_h, gain_a, cache_sb_k[l], cache_sb_v[l])
                s0 = state_hgrn[l]
            o_b, s_new = hgrn2(q_b, f_b, i_b, g_b, lb, s0, gain_b)
            x1, h2, lg = out_projection(o_a, o_b, wo_a, wo_b, x, gate1, scale2, shift2,
                                        ln1_g[l], ln1_b[l], router_w[l], router_b[l], alpha)
            x1s.append(x1)
            h2s.append(h2.reshape(-1, d))
            lgs.append(lg.reshape(-1, n_exp))
            gate2s.append(gate2)
            outs[3 * si + 0].append(k_f)
            outs[3 * si + 1].append(v_f)
            outs[3 * si + 2].append(s_new)

        (slot_tok, router_gates, block_expert, n_used, next_expert, n_valid, sub_live,
         pos) = _route(jnp.concatenate(lgs, axis=0))
        xs = gather_rows(jnp.concatenate(h2s, axis=0), slot_tok, sub_live)
        ys = expert_ffn(xs, block_expert, n_used, next_expert, n_valid, l,
                        w_gate, b_gate[l][:, None, :], w_up, b_up[l][:, None, :],
                        w_down, b_down[l][:, None, :])
        tok0 = 0
        new_streams = []
        for si, x1 in enumerate(x1s):
            n_s = x1.shape[0] * x1.shape[1]
            rg = router_gates[tok0:tok0 + n_s].reshape(x1.shape[0], x1.shape[1], TOP_K)
            new_streams.append(combine(ys, pos, rg, x1, gate2s[si], ln2_g[l], ln2_b[l], alpha, tok0))
            tok0 += n_s
        streams = new_streams

    stack = lambda xs_: jnp.stack(xs_)
    return (streams[0], streams[1], stack(outs[0]), stack(outs[1]), stack(outs[2]),
            stack(outs[3]), stack(outs[4]), stack(outs[5]))
```
